```python
import jax, jax.numpy as jnp
from jax import lax
import numpy as np

D_MODEL = 4096
BATCH = 2
SEQ = 4096
DEPTH = 2

CHUNK = 64
MIX_WIDTH = D_MODEL
GDN_HEAD_DIM = 128
GDN_WIDTH = MIX_WIDTH // 2
GDN_HEADS = GDN_WIDTH // GDN_HEAD_DIM
QKV_WIDTH = 3 * GDN_WIDTH
SHORT_CONV = 4
CONV_WIDTH = MIX_WIDTH - GDN_WIDTH
DW_KERNEL = 31
N_IN = QKV_WIDTH + GDN_WIDTH + 2 * GDN_HEADS + 2 * CONV_WIDTH
D_FF = ((8 * D_MODEL // 3 + 255) // 256) * 256
RMS_EPS = 1e-6
LN_EPS = 1e-5

kernel_name = "hybrid_gdn_conformer_conv_parallel"


def rmsnorm(x, w, eps=RMS_EPS):
    xf = x.astype(jnp.float32)
    y = xf * lax.rsqrt(jnp.mean(xf * xf, axis=-1, keepdims=True) + eps)
    return (y * w.astype(jnp.float32)).astype(x.dtype)


def layernorm(x, w, b, eps=LN_EPS):
    xf = x.astype(jnp.float32)
    mu = jnp.mean(xf, axis=-1, keepdims=True)
    var = jnp.mean(jnp.square(xf - mu), axis=-1, keepdims=True)
    y = (xf - mu) * lax.rsqrt(var + eps)
    return (y * w.astype(jnp.float32) + b.astype(jnp.float32)).astype(x.dtype)


def l2norm(x, eps=1e-6):
    return x * lax.rsqrt(jnp.sum(x * x, axis=-1, keepdims=True) + eps)


def causal_depthwise_conv(x, w):
    width, chans = w.shape
    return lax.conv_general_dilated(
        x, w[:, None, :].astype(x.dtype), window_strides=(1,), padding=[(width - 1, 0)],
        dimension_numbers=("NWC", "WIO", "NWC"), feature_group_count=chans)


def gated_delta_rule(q, k, v, g, beta):
    bsz, t_len, n_h, dk = q.shape
    dv = v.shape[-1]
    n_c = t_len // CHUNK
    q = l2norm(q.astype(jnp.float32)) * (dk ** -0.5)
    k = l2norm(k.astype(jnp.float32))
    v = v.astype(jnp.float32)

    def chunks(t):
        return t.reshape(bsz, n_c, CHUNK, n_h, -1).transpose(0, 3, 1, 2, 4)

    qc, kc, vc = chunks(q), chunks(k), chunks(v)
    gc = g.astype(jnp.float32).reshape(bsz, n_c, CHUNK, n_h).transpose(0, 3, 1, 2)
    bc = beta.astype(jnp.float32).reshape(bsz, n_c, CHUNK, n_h).transpose(0, 3, 1, 2)
    g_cum = jnp.cumsum(gc, axis=-1)

    causal = jnp.tril(jnp.ones((CHUNK, CHUNK), dtype=bool))
    strict = jnp.tril(jnp.ones((CHUNK, CHUNK), dtype=bool), k=-1)
    diff = g_cum[..., :, None] - g_cum[..., None, :]
    decay = jnp.exp(jnp.where(causal, diff, -jnp.inf))

    k_beta = kc * bc[..., None]
    lower = jnp.where(strict, jnp.einsum("bhnid,bhnjd->bhnij", k_beta, kc) * decay, 0.0)
    eye = jnp.eye(CHUNK, dtype=jnp.float32)
    t_inv = lax.linalg.triangular_solve(eye + lower, jnp.broadcast_to(eye, lower.shape),
                                        left_side=True, lower=True, unit_diagonal=True)
    u = jnp.einsum("bhnij,bhnjd->bhnid", t_inv, vc * bc[..., None])
    w = jnp.einsum("bhnij,bhnjd->bhnid", t_inv, k_beta * jnp.exp(g_cum)[..., None])
    a_intra = jnp.einsum("bhnid,bhnjd->bhnij", qc, kc) * decay

    def step(state, inp):
        q_i, k_i, u_i, w_i, g_i, a_i = inp
        v_new = u_i - jnp.einsum("bhcd,bhde->bhce", w_i, state)
        o_i = (jnp.einsum("bhcd,bhde->bhce", q_i * jnp.exp(g_i)[..., None], state)
               + jnp.einsum("bhij,bhje->bhie", a_i, v_new))
        g_last = g_i[..., -1]
        state = (state * jnp.exp(g_last)[..., None, None]
                 + jnp.einsum("bhcd,bhce->bhde", k_i * jnp.exp(g_last[..., None] - g_i)[..., None], v_new))
        return state, o_i

    mv = lambda t: jnp.moveaxis(t, 2, 0)
    s0 = jnp.zeros((bsz, n_h, dk, dv), jnp.float32)
    _, out = lax.scan(step, s0, (mv(qc), mv(kc), mv(u), mv(w), mv(g_cum), mv(a_intra)))
    return out.transpose(1, 0, 3, 2, 4).reshape(bsz, t_len, n_h, dv)


def hybrid_layer(x, pre_mix_norm, w_in, gdn_conv_w, gdn_a_log, gdn_dt_bias, gdn_norm_w,
                 cm_pw_b, cm_dw_w, cm_dw_b, cm_ln_w, cm_ln_b, w_out, post_mix_norm,
                 pre_ffn_norm, w_gate, w_up, w_down, post_ffn_norm):
    bsz, t_len, _ = x.shape
    h = rmsnorm(x, pre_mix_norm)
    proj = h @ w_in
    o1 = QKV_WIDTH
    o2 = o1 + GDN_WIDTH
    o3 = o2 + GDN_HEADS
    o4 = o3 + GDN_HEADS
    qkv, z, b_logit, a_logit, glu_in = jnp.split(proj, [o1, o2, o3, o4], axis=-1)

    qkv = jax.nn.silu(causal_depthwise_conv(qkv, gdn_conv_w))
    q, k, v = jnp.split(qkv, 3, axis=-1)
    heads = lambda t: t.reshape(bsz, t_len, GDN_HEADS, GDN_HEAD_DIM)
    beta = jax.nn.sigmoid(b_logit.astype(jnp.float32))
    g = -jnp.exp(gdn_a_log.astype(jnp.float32)) * jax.nn.softplus(
        a_logit.astype(jnp.float32) + gdn_dt_bias.astype(jnp.float32))
    o_a = gated_delta_rule(heads(q), heads(k), heads(v), g, beta)
    o_a = rmsnorm(o_a, gdn_norm_w) * jax.nn.silu(heads(z).astype(jnp.float32))
    o_a = o_a.reshape(bsz, t_len, GDN_WIDTH).astype(x.dtype)

    c_val, c_gate = jnp.split(glu_in + cm_pw_b, 2, axis=-1)
    c = c_val * jax.nn.sigmoid(c_gate)
    c = causal_depthwise_conv(c, cm_dw_w) + cm_dw_b
    c = jax.nn.silu(layernorm(c, cm_ln_w, cm_ln_b))

    mix = jnp.concatenate([o_a, c], axis=-1) @ w_out
    x = x + rmsnorm(mix, post_mix_norm)

    hf = rmsnorm(x, pre_ffn_norm)
    ff = (jax.nn.silu(hf @ w_gate) * (hf @ w_up)) @ w_down
    return x + rmsnorm(ff, post_ffn_norm)


def setup_inputs(seed: int = 0) -> dict:
    key = jax.random.key(seed)
    ks = jax.random.split(key, 20)
    L = DEPTH
    nrm = lambda k, shape, s: jax.random.normal(k, shape, jnp.float32) * s
    return {
        "x": nrm(ks[0], (BATCH, SEQ, D_MODEL), 1.0),
        "pre_mix_norm": 1.0 + nrm(ks[1], (L, D_MODEL), 0.02),
        "w_in": nrm(ks[2], (L, D_MODEL, N_IN), D_MODEL ** -0.5),
        "gdn_conv_w": nrm(ks[3], (L, SHORT_CONV, QKV_WIDTH), SHORT_CONV ** -0.5),
        "gdn_a_log": jnp.log(jax.random.uniform(ks[4], (L, GDN_HEADS), jnp.float32, 1.0, 16.0)),
        "gdn_dt_bias": nrm(ks[5], (L, GDN_HEADS), 0.1),
        "gdn_norm_w": 1.0 + nrm(ks[6], (L, GDN_HEAD_DIM), 0.02),
        "cm_pw_b": nrm(ks[7], (L, 2 * CONV_WIDTH), 0.02),
        "cm_dw_w": nrm(ks[8], (L, DW_KERNEL, CONV_WIDTH), DW_KERNEL ** -0.5),
        "cm_dw_b": nrm(ks[9], (L, CONV_WIDTH), 0.02),
        "cm_ln_w": 1.0 + nrm(ks[10], (L, CONV_WIDTH), 0.02),
        "cm_ln_b": nrm(ks[11], (L, CONV_WIDTH), 0.02),
        "w_out": nrm(ks[12], (L, MIX_WIDTH, D_MODEL), MIX_WIDTH ** -0.5),
        "post_mix_norm": 1.0 + nrm(ks[13], (L, D_MODEL), 0.02),
        "pre_ffn_norm": 1.0 + nrm(ks[14], (L, D_MODEL), 0.02),
        "w_gate": nrm(ks[15], (L, D_MODEL, D_FF), D_MODEL ** -0.5),
        "w_up": nrm(ks[16], (L, D_MODEL, D_FF), D_MODEL ** -0.5),
        "w_down": nrm(ks[17], (L, D_FF, D_MODEL), D_FF ** -0.5),
        "post_ffn_norm": 1.0 + nrm(ks[18], (L, D_MODEL), 0.02),
    }


def reference(x, pre_mix_norm, w_in, gdn_conv_w, gdn_a_log, gdn_dt_bias, gdn_norm_w,
              cm_pw_b, cm_dw_w, cm_dw_b, cm_ln_w, cm_ln_b, w_out, post_mix_norm,
              pre_ffn_norm, w_gate, w_up, w_down, post_ffn_norm):
    for l in range(DEPTH):
        x = hybrid_layer(x, pre_mix_norm[l], w_in[l], gdn_conv_w[l], gdn_a_log[l], gdn_dt_bias[l],
                         gdn_norm_w[l], cm_pw_b[l], cm_dw_w[l], cm_dw_b[l], cm_ln_w[l], cm_ln_b[l],
                         w_out[l], post_mix_norm[l], pre_ffn_norm[l], w_gate[l], w_up[l], w_down[l],
                         post_ffn_norm[l])
    return x
```

```python
import functools

import jax
import jax.numpy as jnp
from jax import lax
from jax.experimental import pallas as pl
from jax.experimental.pallas import tpu as pltpu

D_MODEL = 4096
CHUNK = 64
HEAD_DIM = 128
GDN_WIDTH = D_MODEL // 2
GDN_HEADS = GDN_WIDTH // HEAD_DIM
QKV_WIDTH = 3 * GDN_WIDTH
SHORT_CONV = 4
CONV_WIDTH = D_MODEL - GDN_WIDTH
DW_KERNEL = 31
RMS_EPS = 1e-6
LN_EPS = 1e-5
L2_EPS = 1e-6

LANES = 128
VMEM_LIMIT = 56 * 1024 * 1024

F32 = jnp.float32
BF16 = jnp.bfloat16


def _params(sem):
    return pltpu.CompilerParams(dimension_semantics=sem, vmem_limit_bytes=VMEM_LIMIT)


def _rms_kernel(x_ref, w_ref, h_ref):
    x = x_ref[...]
    r = lax.rsqrt(jnp.mean(x * x, axis=-1, keepdims=True) + RMS_EPS)
    h_ref[...] = (x * r * w_ref[...]).astype(h_ref.dtype)


def rmsnorm_bf16(x, w, tr=256):
    m, d = x.shape
    return pl.pallas_call(
        _rms_kernel,
        grid=(m // tr,),
        in_specs=[pl.BlockSpec((tr, d), lambda i: (i, 0)),
                  pl.BlockSpec((1, d), lambda i: (0, 0))],
        out_specs=pl.BlockSpec((tr, d), lambda i: (i, 0)),
        out_shape=jax.ShapeDtypeStruct((m, d), BF16),
        compiler_params=_params(("arbitrary",)),
        name="rmsnorm",
    )(x, w.reshape(1, d))


def _mm_kernel(a_ref, w_ref, o_ref):
    o_ref[...] = jnp.dot(a_ref[...], w_ref[...], preferred_element_type=F32).astype(o_ref.dtype)


def matmul(a, w, tm, tn, out_dtype=F32, name="matmul"):
    m, k = a.shape
    _, n = w.shape
    return pl.pallas_call(
        _mm_kernel,
        grid=(m // tm, n // tn),
        in_specs=[pl.BlockSpec((tm, k), lambda i, j: (i, 0)),
                  pl.BlockSpec((k, tn), lambda i, j: (0, j))],
        out_specs=pl.BlockSpec((tm, tn), lambda i, j: (i, j)),
        out_shape=jax.ShapeDtypeStruct((m, n), out_dtype),
        compiler_params=_params(("arbitrary", "arbitrary")),
        name=name,
    )(a, w)


def _mm2_kernel(a1_ref, a2_ref, w1_ref, w2_ref, o_ref):
    acc = jnp.dot(a1_ref[...], w1_ref[...], preferred_element_type=F32)
    acc += jnp.dot(a2_ref[...], w2_ref[...], preferred_element_type=F32)
    o_ref[...] = acc


def matmul_cat(a1, a2, w, tm, tn, name="matmul_cat"):
    m, k1 = a1.shape
    _, k2 = a2.shape
    _, n = w.shape
    assert k1 == k2
    return pl.pallas_call(
        _mm2_kernel,
        grid=(m // tm, n // tn),
        in_specs=[pl.BlockSpec((tm, k1), lambda i, j: (i, 0)),
                  pl.BlockSpec((tm, k2), lambda i, j: (i, 0)),
                  pl.BlockSpec((k1, tn), lambda i, j: (0, j)),
                  pl.BlockSpec((k2, tn), lambda i, j: (1, j))],
        out_specs=pl.BlockSpec((tm, tn), lambda i, j: (i, j)),
        out_shape=jax.ShapeDtypeStruct((m, n), F32),
        compiler_params=_params(("arbitrary", "arbitrary")),
        name=name,
    )(a1, a2, w, w)


def _ffn_up_kernel(h_ref, wg_ref, wu_ref, o_ref):
    h = h_ref[...]
    g = jnp.dot(h, wg_ref[...], preferred_element_type=F32)
    u = jnp.dot(h, wu_ref[...], preferred_element_type=F32)
    o_ref[...] = (g * jax.nn.sigmoid(g) * u).astype(o_ref.dtype)


def ffn_up(h, wg, wu, tm, tn):
    m, k = h.shape
    _, n = wg.shape
    return pl.pallas_call(
        _ffn_up_kernel,
        grid=(m // tm, n // tn),
        in_specs=[pl.BlockSpec((tm, k), lambda i, j: (i, 0)),
                  pl.BlockSpec((k, tn), lambda i, j: (0, j)),
                  pl.BlockSpec((k, tn), lambda i, j: (0, j))],
        out_specs=pl.BlockSpec((tm, tn), lambda i, j: (i, j)),
        out_shape=jax.ShapeDtypeStruct((m, n), BF16),
        compiler_params=_params(("arbitrary", "arbitrary")),
        name="ffn_up",
    )(h, wg, wu)


def _post_kernel(y_ref, x_ref, wp_ref, wn_ref, xo_ref, h_ref):
    y = y_ref[...]
    r = lax.rsqrt(jnp.mean(y * y, axis=-1, keepdims=True) + RMS_EPS)
    xn = x_ref[...] + y * r * wp_ref[...]
    xo_ref[...] = xn
    r2 = lax.rsqrt(jnp.mean(xn * xn, axis=-1, keepdims=True) + RMS_EPS)
    h_ref[...] = (xn * r2 * wn_ref[...]).astype(h_ref.dtype)


def _post_last_kernel(y_ref, x_ref, wp_ref, xo_ref):
    y = y_ref[...]
    r = lax.rsqrt(jnp.mean(y * y, axis=-1, keepdims=True) + RMS_EPS)
    xo_ref[...] = x_ref[...] + y * r * wp_ref[...]


def residual_norm(y, x, w_post, w_next=None, tr=256):
    m, d = x.shape
    row = pl.BlockSpec((tr, d), lambda i: (i, 0))
    vec = pl.BlockSpec((1, d), lambda i: (0, 0))
    if w_next is None:
        return pl.pallas_call(
            _post_last_kernel, grid=(m // tr,),
            in_specs=[row, row, vec], out_specs=row,
            out_shape=jax.ShapeDtypeStruct((m, d), F32),
            compiler_params=_params(("arbitrary",)), name="residual_norm_last",
        )(y, x, w_post.reshape(1, d))
    return pl.pallas_call(
        _post_kernel, grid=(m // tr,),
        in_specs=[row, row, vec, vec], out_specs=[row, row],
        out_shape=[jax.ShapeDtypeStruct((m, d), F32), jax.ShapeDtypeStruct((m, d), BF16)],
        compiler_params=_params(("arbitrary",)), name="residual_norm",
    )(y, x, w_post.reshape(1, d), w_next.reshape(1, d))


def _dot_nt(a, b):
    return lax.dot_general(a, b, (((1,), (1,)), ((), ())), preferred_element_type=F32)


def _dot_tn(a, b):
    return lax.dot_general(a, b, (((0,), (0,)), ((), ())), preferred_element_type=F32)


def _dot_f32(a, b):
    return jnp.dot(a, b, precision=lax.Precision.HIGHEST, preferred_element_type=F32)


def _gdn_kernel(q_ref, k_ref, v_ref, z_ref, ba_ref, cwq_ref, cwk_ref, cwv_ref,
                alog_ref, dtb_ref, nw_ref, o_ref,
                s_ref, qpad, kpad, vpad, qa, ka, va, gc_ref, bt_ref, *, hb, tb):
    hg = pl.program_id(1)
    t = pl.program_id(2)
    nck = tb // CHUNK
    pad0 = 8

    @pl.when(t == 0)
    def _():
        s_ref[...] = jnp.zeros_like(s_ref)
        qpad[0:pad0, :] = jnp.zeros((pad0, hb * HEAD_DIM), F32)
        kpad[0:pad0, :] = jnp.zeros((pad0, hb * HEAD_DIM), F32)
        vpad[0:pad0, :] = jnp.zeros((pad0, hb * HEAD_DIM), F32)

    def conv_silu(pad, x_ref, cw_ref, dst):
        pad[pad0:pad0 + tb, :] = x_ref[...]
        acc = None
        for j in range(SHORT_CONV):
            off = pad0 - (SHORT_CONV - 1) + j
            term = pad[off:off + tb, :] * cw_ref[j:j + 1, :]
            acc = term if acc is None else acc + term
        dst[...] = acc * jax.nn.sigmoid(acc)
        pad[0:pad0, :] = pad[tb:tb + pad0, :]

    conv_silu(qpad, q_ref, cwq_ref, qa)
    conv_silu(kpad, k_ref, cwk_ref, ka)
    conv_silu(vpad, v_ref, cwv_ref, va)

    ba = ba_ref[...]
    bt_ref[...] = jax.nn.sigmoid(ba)
    x = ba + dtb_ref[...]
    softplus = jnp.maximum(x, 0.0) + jnp.log1p(jnp.exp(-jnp.abs(x)))
    g_all = -jnp.exp(alog_ref[...]) * softplus
    ri = lax.broadcasted_iota(jnp.int32, (tb, tb), 0)
    ci = lax.broadcasted_iota(jnp.int32, (tb, tb), 1)
    tri = jnp.where((ci <= ri) & ((ri // CHUNK) == (ci // CHUNK)), 1.0, 0.0).astype(F32)
    gc_ref[...] = _dot_f32(tri, g_all)

    ii = lax.broadcasted_iota(jnp.int32, (CHUNK, CHUNK), 0)
    jj = lax.broadcasted_iota(jnp.int32, (CHUNK, CHUNK), 1)
    causal = jj <= ii
    strict = jj < ii
    eye = jnp.where(ii == jj, 1.0, 0.0).astype(F32)
    lane_i = lax.broadcasted_iota(jnp.int32, (CHUNK, LANES), 1)
    sub_i = lax.broadcasted_iota(jnp.int32, (LANES, CHUNK), 0)
    nw = nw_ref[...]

    def chunk_body(c, carry):
        r0 = pl.multiple_of(c * CHUNK, CHUNK)
        gch = gc_ref[pl.ds(r0, CHUNK), :]
        bch = bt_ref[pl.ds(r0, CHUNK), :]
        gch_t = gch.T
        bch_t = bch.T
        for i in range(hb):
            head = hg * hb + i
            cs = slice(i * HEAD_DIM, (i + 1) * HEAD_DIM)
            gcol = jnp.sum(jnp.where(lane_i == GDN_HEADS + head, gch, 0.0), axis=1, keepdims=True)
            bcol = jnp.sum(jnp.where(lane_i == head, bch, 0.0), axis=1, keepdims=True)
            grow = jnp.sum(jnp.where(sub_i == GDN_HEADS + head, gch_t, 0.0), axis=0, keepdims=True)
            glast = gcol[CHUNK - 1:CHUNK, :]

            q = qa[pl.ds(r0, CHUNK), cs]
            k = ka[pl.ds(r0, CHUNK), cs]
            v = va[pl.ds(r0, CHUNK), cs]
            q = q * lax.rsqrt(jnp.sum(q * q, axis=-1, keepdims=True) + L2_EPS) * (HEAD_DIM ** -0.5)
            k = k * lax.rsqrt(jnp.sum(k * k, axis=-1, keepdims=True) + L2_EPS)
            kb16 = k.astype(BF16)

            decay = jnp.exp(jnp.where(causal, gcol - grow, -jnp.inf))
            kbeta = k * bcol
            qk_kk = _dot_nt(jnp.concatenate([q, kbeta], axis=0).astype(BF16), kb16)
            a_intra = qk_kk[:CHUNK] * decay
            lower = jnp.where(strict, qk_kk[CHUNK:] * decay, 0.0)

            a_pow = -lower
            t_inv = eye + a_pow
            span = 2
            while span < CHUNK:
                a_pow = _dot_f32(a_pow, a_pow)
                t_inv = t_inv + _dot_f32(t_inv, a_pow)
                span *= 2

            eg = jnp.exp(gcol)
            rhs = jnp.concatenate([v * bcol, kbeta * eg], axis=1).astype(BF16)
            uw = jnp.dot(t_inv.astype(BF16), rhs, preferred_element_type=F32)
            u = uw[:, :HEAD_DIM]
            w = uw[:, HEAD_DIM:]

            s = s_ref[i]
            wq = jnp.concatenate([w, q * eg], axis=0).astype(BF16)
            ws_qs = jnp.dot(wq, s.astype(BF16), preferred_element_type=F32)
            v_new = u - ws_qs[:CHUNK]
            vn16 = v_new.astype(BF16)
            o = ws_qs[CHUNK:] + jnp.dot(a_intra.astype(BF16), vn16, preferred_element_type=F32)
            kd = (k * jnp.exp(glast - gcol)).astype(BF16)
            s_ref[i] = s * jnp.exp(glast) + _dot_tn(kd, vn16)

            zc = z_ref[pl.ds(r0, CHUNK), cs]
            o = o * lax.rsqrt(jnp.mean(o * o, axis=-1, keepdims=True) + RMS_EPS) * nw
            o_ref[pl.ds(r0, CHUNK), cs] = (o * (zc * jax.nn.sigmoid(zc))).astype(o_ref.dtype)
        return carry

    lax.fori_loop(0, nck, chunk_body, 0)


def gdn_heads(qkvz, ba, conv_w, a_log_row, dt_bias_row, norm_w, bsz, t_len, hb=2, tb=256):
    m = bsz * t_len
    nt = t_len // tb
    ngrp = GDN_HEADS // hb
    wblk = hb * HEAD_DIM
    kern = functools.partial(_gdn_kernel, hb=hb, tb=tb)

    def col(off):
        return pl.BlockSpec((tb, wblk), lambda b, h, t: (b * nt + t, off * ngrp + h))

    def cw(off):
        return pl.BlockSpec((SHORT_CONV, wblk), lambda b, h, t: (0, off * ngrp + h))

    vec = pl.BlockSpec((1, LANES), lambda b, h, t: (0, 0))
    return pl.pallas_call(
        kern,
        grid=(bsz, ngrp, nt),
        in_specs=[col(0), col(1), col(2), col(3),
                  pl.BlockSpec((tb, LANES), lambda b, h, t: (b * nt + t, 0)),
                  cw(0), cw(1), cw(2), vec, vec, vec],
        out_specs=pl.BlockSpec((tb, wblk), lambda b, h, t: (b * nt + t, h)),
        out_shape=jax.ShapeDtypeStruct((m, GDN_WIDTH), BF16),
        scratch_shapes=[
            pltpu.VMEM((hb, HEAD_DIM, HEAD_DIM), F32),
            pltpu.VMEM((tb + 8, wblk), F32),
            pltpu.VMEM((tb + 8, wblk), F32),
            pltpu.VMEM((tb + 8, wblk), F32),
            pltpu.VMEM((tb, wblk), F32),
            pltpu.VMEM((tb, wblk), F32),
            pltpu.VMEM((tb, wblk), F32),
            pltpu.VMEM((tb, LANES), F32),
            pltpu.VMEM((tb, LANES), F32),
        ],
        compiler_params=_params(("arbitrary", "arbitrary", "arbitrary")),
        name="gdn_heads",
    )(qkvz, qkvz, qkvz, qkvz, ba, conv_w, conv_w, conv_w, a_log_row, dt_bias_row, norm_w)


def _cconv_kernel(val_ref, gate_ref, bv_ref, bg_ref, dww_ref, dwb_ref, lnw_ref, lnb_ref, o_ref,
                  pad, conv, *, tc, rt):
    t = pl.program_id(1)
    hist = 32
    ncs = CONV_WIDTH // LANES

    @pl.when(t == 0)
    def _():
        pad[0:hist, :] = jnp.zeros((hist, CONV_WIDTH), F32)

    pad[hist:hist + tc, :] = (val_ref[...] + bv_ref[...]) * jax.nn.sigmoid(gate_ref[...] + bg_ref[...])

    def stripe(cs, carry):
        c0 = pl.multiple_of(cs * LANES, LANES)
        for r in range(tc // rt):
            acc = jnp.broadcast_to(dwb_ref[:, pl.ds(c0, LANES)], (rt, LANES))
            for j in range(DW_KERNEL):
                off = r * rt + hist - (DW_KERNEL - 1) + j
                acc = acc + pad[off:off + rt, pl.ds(c0, LANES)] * dww_ref[j:j + 1, pl.ds(c0, LANES)]
            conv[r * rt:(r + 1) * rt, pl.ds(c0, LANES)] = acc
        return carry

    lax.fori_loop(0, ncs, stripe, 0)
    pad[0:hist, :] = pad[tc:tc + hist, :]

    def rows(r, carry):
        r0 = pl.multiple_of(r * rt, rt)
        y = conv[pl.ds(r0, rt), :]
        mu = jnp.mean(y, axis=-1, keepdims=True)
        yc = y - mu
        var = jnp.mean(yc * yc, axis=-1, keepdims=True)
        yn = yc * lax.rsqrt(var + LN_EPS) * lnw_ref[...] + lnb_ref[...]
        o_ref[pl.ds(r0, rt), :] = (yn * jax.nn.sigmoid(yn)).astype(o_ref.dtype)
        return carry

    lax.fori_loop(0, tc // rt, rows, 0)


def conformer_conv(glu, pw_b, dw_w, dw_b, ln_w, ln_b, bsz, t_len, tc=256, rt=32):
    m = bsz * t_len
    nt = t_len // tc
    kern = functools.partial(_cconv_kernel, tc=tc, rt=rt)
    half = lambda c: pl.BlockSpec((tc, CONV_WIDTH), lambda b, t: (b * nt + t, c))
    bias = lambda c: pl.BlockSpec((1, CONV_WIDTH), lambda b, t: (0, c))
    vec = pl.BlockSpec((1, CONV_WIDTH), lambda b, t: (0, 0))
    return pl.pallas_call(
        kern,
        grid=(bsz, nt),
        in_specs=[half(0), half(1), bias(0), bias(1),
                  pl.BlockSpec((DW_KERNEL, CONV_WIDTH), lambda b, t: (0, 0)), vec, vec, vec],
        out_specs=pl.BlockSpec((tc, CONV_WIDTH), lambda b, t: (b * nt + t, 0)),
        out_shape=jax.ShapeDtypeStruct((m, CONV_WIDTH), BF16),
        scratch_shapes=[pltpu.VMEM((tc + 32, CONV_WIDTH), F32), pltpu.VMEM((tc, CONV_WIDTH), F32)],
        compiler_params=_params(("arbitrary", "arbitrary")),
        name="conformer_conv",
    )(glu, glu, pw_b.reshape(1, -1), pw_b.reshape(1, -1), dw_w,
      dw_b.reshape(1, -1), ln_w.reshape(1, -1), ln_b.reshape(1, -1))


def _pad_lanes(v, offset):
    out = jnp.zeros((1, LANES), F32)
    return lax.dynamic_update_slice(out, v.reshape(1, -1).astype(F32), (0, offset))


def kernel(x, pre_mix_norm, w_in, gdn_conv_w, gdn_a_log, gdn_dt_bias, gdn_norm_w, cm_pw_b, cm_dw_w,
           cm_dw_b, cm_ln_w, cm_ln_b, w_out, post_mix_norm, pre_ffn_norm, w_gate, w_up, w_down,
           post_ffn_norm):
    bsz, t_len, d = x.shape
    depth = w_in.shape[0]
    m = bsz * t_len
    xf = x.reshape(m, d)
    o_z = QKV_WIDTH + GDN_WIDTH
    o_g = o_z + 2 * GDN_HEADS

    h = rmsnorm_bf16(xf, pre_mix_norm[0])
    for l in range(depth):
        w_qkvz = w_in[l, :, :o_z].astype(BF16)
        w_ba = jnp.pad(w_in[l, :, o_z:o_g], ((0, 0), (0, LANES - 2 * GDN_HEADS))).astype(BF16)
        w_glu = w_in[l, :, o_g:].astype(BF16)

        qkvz = matmul(h, w_qkvz, 1024, 512, name="in_proj_qkvz")
        ba = matmul(h, w_ba, 1024, LANES, name="in_proj_gates")
        glu = matmul(h, w_glu, 1024, 512, name="in_proj_glu")

        o_a = gdn_heads(qkvz, ba, gdn_conv_w[l], _pad_lanes(gdn_a_log[l], GDN_HEADS),
                        _pad_lanes(gdn_dt_bias[l], GDN_HEADS), gdn_norm_w[l].reshape(1, HEAD_DIM),
                        bsz, t_len)
        c = conformer_conv(glu, cm_pw_b[l], cm_dw_w[l], cm_dw_b[l], cm_ln_w[l], cm_ln_b[l], bsz, t_len)

        mix = matmul_cat(o_a, c, w_out[l].astype(BF16), 1024, 512, name="out_proj")
        xf, hf = residual_norm(mix, xf, post_mix_norm[l], pre_ffn_norm[l])

        act = ffn_up(hf, w_gate[l].astype(BF16), w_up[l].astype(BF16), 1024, 256)
        ff = matmul(act, w_down[l].astype(BF16), 512, 256, name="ffn_down")
        if l + 1 < depth:
            xf, h = residual_norm(ff, xf, post_ffn_norm[l], pre_mix_norm[l + 1])
        else:
            xf = residual_norm(ff, xf, post_ffn_norm[l])
    return xf.reshape(bsz, t_len, d)
```

```python
import functools

import jax
import jax.numpy as jnp
from jax import lax
from jax.experimental import pallas as pl
from jax.experimental.pallas import tpu as pltpu

D_MODEL = 4096
CHUNK = 64
HEAD_DIM = 128
GDN_WIDTH = D_MODEL // 2
GDN_HEADS = GDN_WIDTH // HEAD_DIM
QKV_WIDTH = 3 * GDN_WIDTH
SHORT_CONV = 4
CONV_WIDTH = D_MODEL - GDN_WIDTH
DW_KERNEL = 31
RMS_EPS = 1e-6
LN_EPS = 1e-5
L2_EPS = 1e-6

LANES = 128
VMEM_LIMIT = 56 * 1024 * 1024

F32 = jnp.float32
BF16 = jnp.bfloat16


def _params(sem):
    return pltpu.CompilerParams(dimension_semantics=sem, vmem_limit_bytes=VMEM_LIMIT)


def _rms_kernel(x_ref, w_ref, h_ref):
    x = x_ref[...]
    r = lax.rsqrt(jnp.mean(x * x, axis=-1, keepdims=True) + RMS_EPS)
    h_ref[...] = (x * r * w_ref[...]).astype(h_ref.dtype)


def rmsnorm_bf16(x, w, tr=256):
    m, d = x.shape
    return pl.pallas_call(
        _rms_kernel,
        grid=(m // tr,),
        in_specs=[pl.BlockSpec((tr, d), lambda i: (i, 0)),
                  pl.BlockSpec((1, d), lambda i: (0, 0))],
        out_specs=pl.BlockSpec((tr, d), lambda i: (i, 0)),
        out_shape=jax.ShapeDtypeStruct((m, d), BF16),
        compiler_params=_params(("arbitrary",)),
        name="rmsnorm",
    )(x, w.reshape(1, d))


def _mm_kernel(a_ref, w_ref, o_ref):
    o_ref[...] = jnp.dot(a_ref[...], w_ref[...], preferred_element_type=F32).astype(o_ref.dtype)


def matmul(a, w, tm, tn, out_dtype=F32, name="matmul"):
    m, k = a.shape
    _, n = w.shape
    return pl.pallas_call(
        _mm_kernel,
        grid=(m // tm, n // tn),
        in_specs=[pl.BlockSpec((tm, k), lambda i, j: (i, 0)),
                  pl.BlockSpec((k, tn), lambda i, j: (0, j))],
        out_specs=pl.BlockSpec((tm, tn), lambda i, j: (i, j)),
        out_shape=jax.ShapeDtypeStruct((m, n), out_dtype),
        compiler_params=_params(("arbitrary", "arbitrary")),
        name=name,
    )(a, w)


def _mm_ws_kernel(a_ref, w_ref, o_ref, wbf):
    @pl.when(pl.program_id(1) == 0)
    def _():
        wbf[...] = w_ref[...].astype(BF16)

    o_ref[...] = jnp.dot(a_ref[...], wbf[...], preferred_element_type=F32).astype(o_ref.dtype)


def matmul_ws(a, w_stack, layer, n, tm, tn, name):
    m, k = a.shape
    return pl.pallas_call(
        _mm_ws_kernel,
        grid=(n // tn, m // tm),
        in_specs=[pl.BlockSpec((tm, k), lambda j, i: (i, 0)),
                  pl.BlockSpec((None, k, tn), lambda j, i: (layer, 0, j))],
        out_specs=pl.BlockSpec((tm, tn), lambda j, i: (i, j)),
        out_shape=jax.ShapeDtypeStruct((m, n), F32),
        scratch_shapes=[pltpu.VMEM((k, tn), BF16)],
        compiler_params=_params(("arbitrary", "arbitrary")),
        name=name,
    )(a, w_stack)


def _mm2_ws_kernel(a1_ref, a2_ref, w1_ref, w2_ref, o_ref, wbf1, wbf2):
    @pl.when(pl.program_id(1) == 0)
    def _():
        wbf1[...] = w1_ref[...].astype(BF16)
        wbf2[...] = w2_ref[...].astype(BF16)

    acc = jnp.dot(a1_ref[...], wbf1[...], preferred_element_type=F32)
    acc += jnp.dot(a2_ref[...], wbf2[...], preferred_element_type=F32)
    o_ref[...] = acc


def matmul_cat_ws(a1, a2, w_stack, layer, tm, tn, name):
    m, k1 = a1.shape
    _, k2 = a2.shape
    n = w_stack.shape[-1]
    assert k1 == k2
    return pl.pallas_call(
        _mm2_ws_kernel,
        grid=(n // tn, m // tm),
        in_specs=[pl.BlockSpec((tm, k1), lambda j, i: (i, 0)),
                  pl.BlockSpec((tm, k2), lambda j, i: (i, 0)),
                  pl.BlockSpec((None, k1, tn), lambda j, i: (layer, 0, j)),
                  pl.BlockSpec((None, k2, tn), lambda j, i: (layer, 1, j))],
        out_specs=pl.BlockSpec((tm, tn), lambda j, i: (i, j)),
        out_shape=jax.ShapeDtypeStruct((m, n), F32),
        scratch_shapes=[pltpu.VMEM((k1, tn), BF16), pltpu.VMEM((k2, tn), BF16)],
        compiler_params=_params(("arbitrary", "arbitrary")),
        name=name,
    )(a1, a2, w_stack, w_stack)


def _ffn_up_kernel(h_ref, wg_ref, wu_ref, o_ref, wgbf, wubf):
    @pl.when(pl.program_id(1) == 0)
    def _():
        wgbf[...] = wg_ref[...].astype(BF16)
        wubf[...] = wu_ref[...].astype(BF16)

    h = h_ref[...]
    g = jnp.dot(h, wgbf[...], preferred_element_type=F32)
    u = jnp.dot(h, wubf[...], preferred_element_type=F32)
    o_ref[...] = (g * jax.nn.sigmoid(g) * u).astype(o_ref.dtype)


def ffn_up(h, wg_stack, wu_stack, layer, tm, tn):
    m, k = h.shape
    n = wg_stack.shape[-1]
    wspec = pl.BlockSpec((None, k, tn), lambda j, i: (layer, 0, j))
    return pl.pallas_call(
        _ffn_up_kernel,
        grid=(n // tn, m // tm),
        in_specs=[pl.BlockSpec((tm, k), lambda j, i: (i, 0)), wspec, wspec],
        out_specs=pl.BlockSpec((tm, tn), lambda j, i: (i, j)),
        out_shape=jax.ShapeDtypeStruct((m, n), BF16),
        scratch_shapes=[pltpu.VMEM((k, tn), BF16), pltpu.VMEM((k, tn), BF16)],
        compiler_params=_params(("arbitrary", "arbitrary")),
        name="ffn_up",
    )(h, wg_stack, wu_stack)


def _mm_as_kernel(a_ref, w_ref, o_ref):
    o_ref[...] = jnp.dot(a_ref[...], w_ref[...].astype(BF16), preferred_element_type=F32)


def matmul_as(a, w_stack, layer, tm, tn, name):
    m, k = a.shape
    n = w_stack.shape[-1]
    return pl.pallas_call(
        _mm_as_kernel,
        grid=(m // tm, n // tn),
        in_specs=[pl.BlockSpec((tm, k), lambda i, j: (i, 0), pipeline_mode=pl.Buffered(1)),
                  pl.BlockSpec((None, k, tn), lambda i, j: (layer, 0, j))],
        out_specs=pl.BlockSpec((tm, tn), lambda i, j: (i, j)),
        out_shape=jax.ShapeDtypeStruct((m, n), F32),
        compiler_params=_params(("arbitrary", "arbitrary")),
        name=name,
    )(a, w_stack)


def _post_kernel(y_ref, x_ref, wp_ref, wn_ref, xo_ref, h_ref):
    y = y_ref[...]
    r = lax.rsqrt(jnp.mean(y * y, axis=-1, keepdims=True) + RMS_EPS)
    xn = x_ref[...] + y * r * wp_ref[...]
    xo_ref[...] = xn
    r2 = lax.rsqrt(jnp.mean(xn * xn, axis=-1, keepdims=True) + RMS_EPS)
    h_ref[...] = (xn * r2 * wn_ref[...]).astype(h_ref.dtype)


def _post_last_kernel(y_ref, x_ref, wp_ref, xo_ref):
    y = y_ref[...]
    r = lax.rsqrt(jnp.mean(y * y, axis=-1, keepdims=True) + RMS_EPS)
    xo_ref[...] = x_ref[...] + y * r * wp_ref[...]


def residual_norm(y, x, w_post, w_next=None, tr=256):
    m, d = x.shape
    row = pl.BlockSpec((tr, d), lambda i: (i, 0))
    vec = pl.BlockSpec((1, d), lambda i: (0, 0))
    if w_next is None:
        return pl.pallas_call(
            _post_last_kernel, grid=(m // tr,),
            in_specs=[row, row, vec], out_specs=row,
            out_shape=jax.ShapeDtypeStruct((m, d), F32),
            compiler_params=_params(("arbitrary",)), name="residual_norm_last",
        )(y, x, w_post.reshape(1, d))
    return pl.pallas_call(
        _post_kernel, grid=(m // tr,),
        in_specs=[row, row, vec, vec], out_specs=[row, row],
        out_shape=[jax.ShapeDtypeStruct((m, d), F32), jax.ShapeDtypeStruct((m, d), BF16)],
        compiler_params=_params(("arbitrary",)), name="residual_norm",
    )(y, x, w_post.reshape(1, d), w_next.reshape(1, d))


def _dot_nt(a, b):
    return lax.dot_general(a, b, (((1,), (1,)), ((), ())), preferred_element_type=F32)


def _dot_tn(a, b):
    return lax.dot_general(a, b, (((0,), (0,)), ((), ())), preferred_element_type=F32)


def _dot_f32(a, b):
    return jnp.dot(a, b, precision=lax.Precision.HIGHEST, preferred_element_type=F32)


def _gdn_kernel(q_ref, k_ref, v_ref, z_ref, ba_ref, cwq_ref, cwk_ref, cwv_ref,
                alog_ref, dtb_ref, nw_ref, o_ref,
                s_ref, qpad, kpad, vpad, qa, ka, va):
    t = pl.program_id(1)
    pad0 = 8

    @pl.when(t == 0)
    def _():
        s_ref[...] = jnp.zeros_like(s_ref)
        qpad[0:pad0, :] = jnp.zeros((pad0, GDN_WIDTH), F32)
        kpad[0:pad0, :] = jnp.zeros((pad0, GDN_WIDTH), F32)
        vpad[0:pad0, :] = jnp.zeros((pad0, GDN_WIDTH), F32)

    def conv_silu(pad, x_ref, cw_ref, dst):
        pad[pad0:pad0 + CHUNK, :] = x_ref[...]
        acc = None
        for j in range(SHORT_CONV):
            off = pad0 - (SHORT_CONV - 1) + j
            term = pad[off:off + CHUNK, :] * cw_ref[j:j + 1, :]
            acc = term if acc is None else acc + term
        dst[...] = acc * jax.nn.sigmoid(acc)
        pad[0:pad0, :] = pad[CHUNK:CHUNK + pad0, :]

    conv_silu(qpad, q_ref, cwq_ref, qa)
    conv_silu(kpad, k_ref, cwk_ref, ka)
    conv_silu(vpad, v_ref, cwv_ref, va)

    ii = lax.broadcasted_iota(jnp.int32, (CHUNK, CHUNK), 0)
    jj = lax.broadcasted_iota(jnp.int32, (CHUNK, CHUNK), 1)
    causal = jj <= ii
    strict = jj < ii
    eye = jnp.where(ii == jj, 1.0, 0.0).astype(F32)
    level_masks = []
    s_blk = 1
    while s_blk < CHUNK:
        level_masks.append(((ii // (2 * s_blk)) == (jj // (2 * s_blk))) & ((ii // s_blk) != (jj // s_blk)))
        s_blk *= 2
    nw = nw_ref[...]

    def bdot(a, b):
        return jnp.dot(a.astype(BF16), b.astype(BF16), preferred_element_type=F32)

    ba = ba_ref[...]
    beta_all = jax.nn.sigmoid(ba)
    x = ba + dtb_ref[...]
    softplus = jnp.maximum(x, 0.0) + jnp.log1p(jnp.exp(-jnp.abs(x)))
    g_all = -jnp.exp(alog_ref[...]) * softplus
    gcum = _dot_f32(jnp.where(causal, 1.0, 0.0).astype(F32), g_all)
    gcum_t = gcum.T

    def prep(i):
        cs = slice(i * HEAD_DIM, (i + 1) * HEAD_DIM)
        gcol = gcum[:, GDN_HEADS + i:GDN_HEADS + i + 1]
        grow = gcum_t[GDN_HEADS + i:GDN_HEADS + i + 1, :]
        bcol = beta_all[:, i:i + 1]
        glast = gcol[CHUNK - 1:CHUNK, :]
        q = qa[:, cs]
        k = ka[:, cs]
        q = q * (lax.rsqrt(jnp.sum(q * q, axis=-1, keepdims=True) + L2_EPS) * (HEAD_DIM ** -0.5))
        k = k * lax.rsqrt(jnp.sum(k * k, axis=-1, keepdims=True) + L2_EPS)
        eg = jnp.exp(gcol)
        kbeta = k * bcol
        return dict(
            decay=jnp.exp(jnp.where(causal, gcol - grow, -jnp.inf)),
            k16=k.astype(BF16), qkb=jnp.concatenate([q, kbeta], axis=0).astype(BF16),
            rhs=jnp.concatenate([va[:, cs] * bcol, kbeta * eg], axis=1).astype(BF16),
            qg=q * eg, kd=(k * jnp.exp(glast - gcol)).astype(BF16), dlast=jnp.exp(glast))

    pr = [prep(i) for i in range(GDN_HEADS)]
    qk_kk = [_dot_nt(p["qkb"], p["k16"]) for p in pr]
    a_intra = [(x[:CHUNK] * p["decay"]).astype(BF16) for x, p in zip(qk_kk, pr)]
    lower = [jnp.where(strict, x[CHUNK:] * p["decay"], 0.0) for x, p in zip(qk_kk, pr)]

    t_inv = [eye - jnp.where(level_masks[0], lo, 0.0) for lo in lower]
    for mask in level_masks[1:]:
        y = [bdot(jnp.where(mask, lo, 0.0), t) for lo, t in zip(lower, t_inv)]
        t_inv = [t - bdot(t, yy) for t, yy in zip(t_inv, y)]
    uw = [jnp.dot(t.astype(BF16), p["rhs"], preferred_element_type=F32) for t, p in zip(t_inv, pr)]
    wq = [jnp.concatenate([x[:, HEAD_DIM:], p["qg"]], axis=0).astype(BF16) for x, p in zip(uw, pr)]

    ws_qs = [jnp.dot(wq[i], s_ref[i].astype(BF16), preferred_element_type=F32)
             for i in range(GDN_HEADS)]
    vn16 = [(uw[i][:, :HEAD_DIM] - ws_qs[i][:CHUNK]).astype(BF16) for i in range(GDN_HEADS)]
    av = [jnp.dot(a_intra[i], vn16[i], preferred_element_type=F32) for i in range(GDN_HEADS)]
    kv = [_dot_tn(pr[i]["kd"], vn16[i]) for i in range(GDN_HEADS)]
    for i in range(GDN_HEADS):
        cs = slice(i * HEAD_DIM, (i + 1) * HEAD_DIM)
        s_ref[i] = s_ref[i] * pr[i]["dlast"] + kv[i]
        o = ws_qs[i][CHUNK:] + av[i]
        o = o * lax.rsqrt(jnp.mean(o * o, axis=-1, keepdims=True) + RMS_EPS) * nw
        zc = z_ref[:, cs]
        o_ref[:, cs] = (o * (zc * jax.nn.sigmoid(zc))).astype(o_ref.dtype)


def gdn_heads(qkvz, ba, conv_w, a_log_row, dt_bias_row, norm_w, bsz, t_len):
    m = bsz * t_len
    nt = t_len // CHUNK
    col = lambda off: pl.BlockSpec((CHUNK, GDN_WIDTH), lambda b, t: (b * nt + t, off))
    cw = lambda off: pl.BlockSpec((SHORT_CONV, GDN_WIDTH), lambda b, t: (0, off))
    vec = pl.BlockSpec((1, LANES), lambda b, t: (0, 0))
    return pl.pallas_call(
        _gdn_kernel,
        grid=(bsz, nt),
        in_specs=[col(0), col(1), col(2), col(3),
                  pl.BlockSpec((CHUNK, LANES), lambda b, t: (b * nt + t, 0)),
                  cw(0), cw(1), cw(2), vec, vec, vec],
        out_specs=pl.BlockSpec((CHUNK, GDN_WIDTH), lambda b, t: (b * nt + t, 0)),
        out_shape=jax.ShapeDtypeStruct((m, GDN_WIDTH), BF16),
        scratch_shapes=[
            pltpu.VMEM((GDN_HEADS, HEAD_DIM, HEAD_DIM), F32),
            pltpu.VMEM((CHUNK + 8, GDN_WIDTH), F32),
            pltpu.VMEM((CHUNK + 8, GDN_WIDTH), F32),
            pltpu.VMEM((CHUNK + 8, GDN_WIDTH), F32),
            pltpu.VMEM((CHUNK, GDN_WIDTH), F32),
            pltpu.VMEM((CHUNK, GDN_WIDTH), F32),
            pltpu.VMEM((CHUNK, GDN_WIDTH), F32),
        ],
        compiler_params=_params(("arbitrary", "arbitrary")),
        name="gdn_heads",
    )(qkvz, qkvz, qkvz, qkvz, ba, conv_w, conv_w, conv_w, a_log_row, dt_bias_row, norm_w)


def _cconv_kernel(val_ref, gate_ref, bv_ref, bg_ref, dww_ref, dwb_ref, lnw_ref, lnb_ref, o_ref,
                  pad, conv, *, tc, rt):
    t = pl.program_id(1)
    hist = 32
    ncs = CONV_WIDTH // LANES

    @pl.when(t == 0)
    def _():
        pad[0:hist, :] = jnp.zeros((hist, CONV_WIDTH), F32)

    pad[hist:hist + tc, :] = (val_ref[...] + bv_ref[...]) * jax.nn.sigmoid(gate_ref[...] + bg_ref[...])

    def stripe(cs, carry):
        c0 = pl.multiple_of(cs * LANES, LANES)
        for r in range(tc // rt):
            acc = jnp.broadcast_to(dwb_ref[:, pl.ds(c0, LANES)], (rt, LANES))
            for j in range(DW_KERNEL):
                off = r * rt + hist - (DW_KERNEL - 1) + j
                acc = acc + pad[off:off + rt, pl.ds(c0, LANES)] * dww_ref[j:j + 1, pl.ds(c0, LANES)]
            conv[r * rt:(r + 1) * rt, pl.ds(c0, LANES)] = acc
        return carry

    lax.fori_loop(0, ncs, stripe, 0)
    pad[0:hist, :] = pad[tc:tc + hist, :]

    def rows(r, carry):
        r0 = pl.multiple_of(r * rt, rt)
        y = conv[pl.ds(r0, rt), :]
        mu = jnp.mean(y, axis=-1, keepdims=True)
        yc = y - mu
        var = jnp.mean(yc * yc, axis=-1, keepdims=True)
        yn = yc * lax.rsqrt(var + LN_EPS) * lnw_ref[...] + lnb_ref[...]
        o_ref[pl.ds(r0, rt), :] = (yn * jax.nn.sigmoid(yn)).astype(o_ref.dtype)
        return carry

    lax.fori_loop(0, tc // rt, rows, 0)


def conformer_conv(glu, pw_b, dw_w, dw_b, ln_w, ln_b, bsz, t_len, tc=256, rt=32):
    m = bsz * t_len
    nt = t_len // tc
    kern = functools.partial(_cconv_kernel, tc=tc, rt=rt)
    half = lambda c: pl.BlockSpec((tc, CONV_WIDTH), lambda b, t: (b * nt + t, c))
    bias = lambda c: pl.BlockSpec((1, CONV_WIDTH), lambda b, t: (0, c))
    vec = pl.BlockSpec((1, CONV_WIDTH), lambda b, t: (0, 0))
    return pl.pallas_call(
        kern,
        grid=(bsz, nt),
        in_specs=[half(0), half(1), bias(0), bias(1),
                  pl.BlockSpec((DW_KERNEL, CONV_WIDTH), lambda b, t: (0, 0)), vec, vec, vec],
        out_specs=pl.BlockSpec((tc, CONV_WIDTH), lambda b, t: (b * nt + t, 0)),
        out_shape=jax.ShapeDtypeStruct((m, CONV_WIDTH), BF16),
        scratch_shapes=[pltpu.VMEM((tc + 32, CONV_WIDTH), F32), pltpu.VMEM((tc, CONV_WIDTH), F32)],
        compiler_params=_params(("arbitrary", "arbitrary")),
        name="conformer_conv",
    )(glu, glu, pw_b.reshape(1, -1), pw_b.reshape(1, -1), dw_w,
      dw_b.reshape(1, -1), ln_w.reshape(1, -1), ln_b.reshape(1, -1))


def _pad_lanes(v, offset):
    out = jnp.zeros((1, LANES), F32)
    return lax.dynamic_update_slice(out, v.reshape(1, -1).astype(F32), (0, offset))


def kernel(x, pre_mix_norm, w_in, gdn_conv_w, gdn_a_log, gdn_dt_bias, gdn_norm_w, cm_pw_b, cm_dw_w,
           cm_dw_b, cm_ln_w, cm_ln_b, w_out, post_mix_norm, pre_ffn_norm, w_gate, w_up, w_down,
           post_ffn_norm):
    bsz, t_len, d = x.shape
    depth = w_in.shape[0]
    m = bsz * t_len
    xf = x.reshape(m, d)
    o_z = QKV_WIDTH + GDN_WIDTH
    o_g = o_z + 2 * GDN_HEADS

    h = rmsnorm_bf16(xf, pre_mix_norm[0])
    for l in range(depth):
        w_ba = jnp.pad(w_in[l, :, o_z:o_g], ((0, 0), (0, LANES - 2 * GDN_HEADS))).astype(BF16)
        w_glu = w_in[l, :, o_g:].astype(BF16)

        qkvz = matmul_ws(h, w_in, l, o_z, 1024, 512, name="in_proj_qkvz")
        ba = matmul(h, w_ba, 1024, LANES, name="in_proj_gates")
        glu = matmul(h, w_glu, 1024, 512, name="in_proj_glu")

        o_a = gdn_heads(qkvz, ba, gdn_conv_w[l], _pad_lanes(gdn_a_log[l], GDN_HEADS),
                        _pad_lanes(gdn_dt_bias[l], GDN_HEADS), gdn_norm_w[l].reshape(1, HEAD_DIM),
                        bsz, t_len)
        c = conformer_conv(glu, cm_pw_b[l], cm_dw_w[l], cm_dw_b[l], cm_ln_w[l], cm_ln_b[l], bsz, t_len)

        mix = matmul_cat_ws(o_a, c, w_out, l, 1024, 512, name="out_proj")
        xf, hf = residual_norm(mix, xf, post_mix_norm[l], pre_ffn_norm[l])

        act = ffn_up(hf, w_gate, w_up, l, 1024, 256)
        ff = matmul_as(act, w_down, l, 1024, 256, name="ffn_down")
        if l + 1 < depth:
            xf, h = residual_norm(ff, xf, post_ffn_norm[l], pre_mix_norm[l + 1])
        else:
            xf = residual_norm(ff, xf, post_ffn_norm[l])
    return xf.reshape(bsz, t_len, d)
```

```python
import functools

import jax
import jax.numpy as jnp
from jax import lax
from jax.experimental import pallas as pl
from jax.experimental.pallas import tpu as pltpu

D_MODEL = 4096
CHUNK = 64
HEAD_DIM = 128
GDN_WIDTH = D_MODEL // 2
GDN_HEADS = GDN_WIDTH // HEAD_DIM
QKV_WIDTH = 3 * GDN_WIDTH
SHORT_CONV = 4
CONV_WIDTH = D_MODEL - GDN_WIDTH
DW_KERNEL = 31
RMS_EPS = 1e-6
LN_EPS = 1e-5
L2_EPS = 1e-6

LANES = 128
SUBLANES = 8
VMEM_LIMIT = 56 * 1024 * 1024

F32 = jnp.float32
BF16 = jnp.bfloat16


def _params(sem):
    return pltpu.CompilerParams(dimension_semantics=sem, vmem_limit_bytes=VMEM_LIMIT)


def _rms_kernel(x_ref, w_ref, h_ref):
    x = x_ref[...]
    r = lax.rsqrt(jnp.mean(x * x, axis=-1, keepdims=True) + RMS_EPS)
    h_ref[...] = (x * r * w_ref[...]).astype(h_ref.dtype)


def rmsnorm_bf16(x, w, tr=256):
    m, d = x.shape
    return pl.pallas_call(
        _rms_kernel,
        grid=(m // tr,),
        in_specs=[pl.BlockSpec((tr, d), lambda i: (i, 0)),
                  pl.BlockSpec((1, d), lambda i: (0, 0))],
        out_specs=pl.BlockSpec((tr, d), lambda i: (i, 0)),
        out_shape=jax.ShapeDtypeStruct((m, d), BF16),
        compiler_params=_params(("arbitrary",)),
        name="rmsnorm",
    )(x, w.reshape(1, d))


def _row_block(tm, k):
    return pl.BlockSpec((tm, k), lambda i, j: (i, 0), pipeline_mode=pl.Buffered(1))


def _dot_nt(a, b):
    return lax.dot_general(a, b, (((1,), (1,)), ((), ())), preferred_element_type=F32)


def _dot_tn(a, b):
    return lax.dot_general(a, b, (((0,), (0,)), ((), ())), preferred_element_type=F32)


def _mm_kernel(a_ref, w_ref, o_ref):
    o_ref[...] = jnp.dot(a_ref[...], w_ref[...].astype(BF16), preferred_element_type=F32)


def _mm_nt_kernel(a_ref, wt_ref, o_ref):
    o_ref[...] = _dot_nt(a_ref[...], wt_ref[...].astype(BF16))


def matmul(a, w_stack, layer, tm, tn, name):
    m, k = a.shape
    n = w_stack.shape[-1]
    return pl.pallas_call(
        _mm_kernel,
        grid=(m // tm, n // tn),
        in_specs=[_row_block(tm, k),
                  pl.BlockSpec((None, k, tn), lambda i, j: (layer, 0, j))],
        out_specs=pl.BlockSpec((tm, tn), lambda i, j: (i, j)),
        out_shape=jax.ShapeDtypeStruct((m, n), F32),
        compiler_params=_params(("arbitrary", "arbitrary")),
        name=name,
    )(a, w_stack)


def matmul_nt(a, wt_stack, layer, n, tm, tn, name):
    m, k = a.shape
    return pl.pallas_call(
        _mm_nt_kernel,
        grid=(m // tm, n // tn),
        in_specs=[_row_block(tm, k),
                  pl.BlockSpec((None, tn, k), lambda i, j: (layer, j, 0))],
        out_specs=pl.BlockSpec((tm, tn), lambda i, j: (i, j)),
        out_shape=jax.ShapeDtypeStruct((m, n), F32),
        compiler_params=_params(("arbitrary", "arbitrary")),
        name=name,
    )(a, wt_stack)


def _mm_cat_kernel(a1_ref, a2_ref, w1_ref, w2_ref, o_ref):
    acc = jnp.dot(a1_ref[...], w1_ref[...].astype(BF16), preferred_element_type=F32)
    acc += jnp.dot(a2_ref[...], w2_ref[...].astype(BF16), preferred_element_type=F32)
    o_ref[...] = acc


def matmul_cat(a1, a2, w_stack, layer, tm, tn, name):
    m, k1 = a1.shape
    _, k2 = a2.shape
    n = w_stack.shape[-1]
    assert k1 == k2
    return pl.pallas_call(
        _mm_cat_kernel,
        grid=(m // tm, n // tn),
        in_specs=[_row_block(tm, k1), _row_block(tm, k2),
                  pl.BlockSpec((None, k1, tn), lambda i, j: (layer, 0, j)),
                  pl.BlockSpec((None, k2, tn), lambda i, j: (layer, 1, j))],
        out_specs=pl.BlockSpec((tm, tn), lambda i, j: (i, j)),
        out_shape=jax.ShapeDtypeStruct((m, n), F32),
        compiler_params=_params(("arbitrary", "arbitrary")),
        name=name,
    )(a1, a2, w_stack, w_stack)


def _ffn_up_kernel(h_ref, wg_ref, wu_ref, o_ref):
    h = h_ref[...]
    g = jnp.dot(h, wg_ref[...].astype(BF16), preferred_element_type=F32)
    u = jnp.dot(h, wu_ref[...].astype(BF16), preferred_element_type=F32)
    o_ref[...] = (g * jax.nn.sigmoid(g) * u).astype(o_ref.dtype)


def ffn_up(h, wg_stack, wu_stack, layer, tm, tn):
    m, k = h.shape
    n = wg_stack.shape[-1]
    wspec = pl.BlockSpec((None, k, tn), lambda i, j: (layer, 0, j))
    return pl.pallas_call(
        _ffn_up_kernel,
        grid=(m // tm, n // tn),
        in_specs=[_row_block(tm, k), wspec, wspec],
        out_specs=pl.BlockSpec((tm, tn), lambda i, j: (i, j)),
        out_shape=jax.ShapeDtypeStruct((m, n), BF16),
        compiler_params=_params(("arbitrary", "arbitrary")),
        name="ffn_up",
    )(h, wg_stack, wu_stack)


def _post_kernel(y_ref, x_ref, wp_ref, wn_ref, xo_ref, h_ref):
    y = y_ref[...]
    r = lax.rsqrt(jnp.mean(y * y, axis=-1, keepdims=True) + RMS_EPS)
    xn = x_ref[...] + y * r * wp_ref[...]
    xo_ref[...] = xn
    r2 = lax.rsqrt(jnp.mean(xn * xn, axis=-1, keepdims=True) + RMS_EPS)
    h_ref[...] = (xn * r2 * wn_ref[...]).astype(h_ref.dtype)


def _post_last_kernel(y_ref, x_ref, wp_ref, xo_ref):
    y = y_ref[...]
    r = lax.rsqrt(jnp.mean(y * y, axis=-1, keepdims=True) + RMS_EPS)
    xo_ref[...] = x_ref[...] + y * r * wp_ref[...]


def residual_norm(y, x, w_post, w_next=None, tr=256):
    m, d = x.shape
    row = pl.BlockSpec((tr, d), lambda i: (i, 0))
    vec = pl.BlockSpec((1, d), lambda i: (0, 0))
    if w_next is None:
        return pl.pallas_call(
            _post_last_kernel, grid=(m // tr,),
            in_specs=[row, row, vec], out_specs=row,
            out_shape=jax.ShapeDtypeStruct((m, d), F32),
            compiler_params=_params(("arbitrary",)), name="residual_norm_last",
        )(y, x, w_post.reshape(1, d))
    return pl.pallas_call(
        _post_kernel, grid=(m // tr,),
        in_specs=[row, row, vec, vec], out_specs=[row, row],
        out_shape=[jax.ShapeDtypeStruct((m, d), F32), jax.ShapeDtypeStruct((m, d), BF16)],
        compiler_params=_params(("arbitrary",)), name="residual_norm",
    )(y, x, w_post.reshape(1, d), w_next.reshape(1, d))


def _dot_f32(a, b):
    return jnp.dot(a, b, precision=lax.Precision.HIGHEST, preferred_element_type=F32)


def _gdn_kernel(q_ref, k_ref, v_ref, z_ref, ba_ref, cwq_ref, cwk_ref, cwv_ref,
                alog_ref, dtb_ref, nw_ref, o_ref,
                s_ref, qpad, kpad, vpad, qa, ka, va):
    t = pl.program_id(1)
    pad0 = SUBLANES

    @pl.when(t == 0)
    def _():
        s_ref[...] = jnp.zeros_like(s_ref)
        qpad[0:pad0, :] = jnp.zeros((pad0, GDN_WIDTH), F32)
        kpad[0:pad0, :] = jnp.zeros((pad0, GDN_WIDTH), F32)
        vpad[0:pad0, :] = jnp.zeros((pad0, GDN_WIDTH), F32)

    def conv_silu(pad, x_ref, cw_ref, dst):
        pad[pad0:pad0 + CHUNK, :] = x_ref[...]
        acc = None
        for j in range(SHORT_CONV):
            off = pad0 - (SHORT_CONV - 1) + j
            term = pad[off:off + CHUNK, :] * cw_ref[j:j + 1, :]
            acc = term if acc is None else acc + term
        dst[...] = acc * jax.nn.sigmoid(acc)
        pad[0:pad0, :] = pad[CHUNK:CHUNK + pad0, :]

    conv_silu(qpad, q_ref, cwq_ref, qa)
    conv_silu(kpad, k_ref, cwk_ref, ka)
    conv_silu(vpad, v_ref, cwv_ref, va)

    ii = lax.broadcasted_iota(jnp.int32, (CHUNK, CHUNK), 0)
    jj = lax.broadcasted_iota(jnp.int32, (CHUNK, CHUNK), 1)
    causal = jj <= ii
    strict = jj < ii
    eye = jnp.where(ii == jj, 1.0, 0.0).astype(F32)
    level_masks = []
    s_blk = 1
    while s_blk < CHUNK:
        level_masks.append(((ii // (2 * s_blk)) == (jj // (2 * s_blk))) & ((ii // s_blk) != (jj // s_blk)))
        s_blk *= 2
    nw = nw_ref[...]

    def bdot(a, b):
        return jnp.dot(a.astype(BF16), b.astype(BF16), preferred_element_type=F32)

    ba = ba_ref[...]
    beta_all = jax.nn.sigmoid(ba)
    x = ba + dtb_ref[...]
    softplus = jnp.maximum(x, 0.0) + jnp.log1p(jnp.exp(-jnp.abs(x)))
    g_all = -jnp.exp(alog_ref[...]) * softplus
    gcum = _dot_f32(jnp.where(causal, 1.0, 0.0).astype(F32), g_all)
    gcum_t = gcum.T

    def prep(i):
        cs = slice(i * HEAD_DIM, (i + 1) * HEAD_DIM)
        gcol = gcum[:, GDN_HEADS + i:GDN_HEADS + i + 1]
        grow = gcum_t[GDN_HEADS + i:GDN_HEADS + i + 1, :]
        bcol = beta_all[:, i:i + 1]
        glast = gcol[CHUNK - 1:CHUNK, :]
        q = qa[:, cs]
        k = ka[:, cs]
        q = q * (lax.rsqrt(jnp.sum(q * q, axis=-1, keepdims=True) + L2_EPS) * (HEAD_DIM ** -0.5))
        k = k * lax.rsqrt(jnp.sum(k * k, axis=-1, keepdims=True) + L2_EPS)
        eg = jnp.exp(gcol)
        kbeta = k * bcol
        return dict(
            decay=jnp.exp(jnp.where(causal, gcol - grow, -jnp.inf)),
            k16=k.astype(BF16), qkb=jnp.concatenate([q, kbeta], axis=0).astype(BF16),
            rhs=jnp.concatenate([va[:, cs] * bcol, kbeta * eg], axis=1).astype(BF16),
            qg=q * eg, kd=(k * jnp.exp(glast - gcol)).astype(BF16), dlast=jnp.exp(glast))

    pr = [prep(i) for i in range(GDN_HEADS)]
    qk_kk = [_dot_nt(p["qkb"], p["k16"]) for p in pr]
    a_intra = [(x[:CHUNK] * p["decay"]).astype(BF16) for x, p in zip(qk_kk, pr)]
    lower = [jnp.where(strict, x[CHUNK:] * p["decay"], 0.0) for x, p in zip(qk_kk, pr)]

    t_inv = [eye - jnp.where(level_masks[0], lo, 0.0) for lo in lower]
    for mask in level_masks[1:]:
        y = [bdot(jnp.where(mask, lo, 0.0), t) for lo, t in zip(lower, t_inv)]
        t_inv = [t - bdot(t, yy) for t, yy in zip(t_inv, y)]
    uw = [jnp.dot(t.astype(BF16), p["rhs"], preferred_element_type=F32) for t, p in zip(t_inv, pr)]
    wq = [jnp.concatenate([x[:, HEAD_DIM:], p["qg"]], axis=0).astype(BF16) for x, p in zip(uw, pr)]

    ws_qs = [jnp.dot(wq[i], s_ref[i].astype(BF16), preferred_element_type=F32)
             for i in range(GDN_HEADS)]
    vn16 = [(uw[i][:, :HEAD_DIM] - ws_qs[i][:CHUNK]).astype(BF16) for i in range(GDN_HEADS)]
    av = [jnp.dot(a_intra[i], vn16[i], preferred_element_type=F32) for i in range(GDN_HEADS)]
    kv = [_dot_tn(pr[i]["kd"], vn16[i]) for i in range(GDN_HEADS)]
    for i in range(GDN_HEADS):
        cs = slice(i * HEAD_DIM, (i + 1) * HEAD_DIM)
        s_ref[i] = s_ref[i] * pr[i]["dlast"] + kv[i]
        o = ws_qs[i][CHUNK:] + av[i]
        o = o * lax.rsqrt(jnp.mean(o * o, axis=-1, keepdims=True) + RMS_EPS) * nw
        zc = z_ref[:, cs]
        o_ref[:, cs] = (o * (zc * jax.nn.sigmoid(zc))).astype(o_ref.dtype)


def gdn_heads(qkvz, ba, conv_w, a_log_row, dt_bias_row, norm_w, bsz, t_len):
    m = bsz * t_len
    nt = t_len // CHUNK
    col = lambda off: pl.BlockSpec((CHUNK, GDN_WIDTH), lambda b, t: (b * nt + t, off))
    cw = lambda off: pl.BlockSpec((SHORT_CONV, GDN_WIDTH), lambda b, t: (0, off))
    vec = pl.BlockSpec((1, LANES), lambda b, t: (0, 0))
    return pl.pallas_call(
        _gdn_kernel,
        grid=(bsz, nt),
        in_specs=[col(0), col(1), col(2), col(3),
                  pl.BlockSpec((CHUNK, LANES), lambda b, t: (b * nt + t, 0)),
                  cw(0), cw(1), cw(2), vec, vec, vec],
        out_specs=pl.BlockSpec((CHUNK, GDN_WIDTH), lambda b, t: (b * nt + t, 0)),
        out_shape=jax.ShapeDtypeStruct((m, GDN_WIDTH), BF16),
        scratch_shapes=[
            pltpu.VMEM((GDN_HEADS, HEAD_DIM, HEAD_DIM), F32),
            pltpu.VMEM((CHUNK + SUBLANES, GDN_WIDTH), F32),
            pltpu.VMEM((CHUNK + SUBLANES, GDN_WIDTH), F32),
            pltpu.VMEM((CHUNK + SUBLANES, GDN_WIDTH), F32),
            pltpu.VMEM((CHUNK, GDN_WIDTH), F32),
            pltpu.VMEM((CHUNK, GDN_WIDTH), F32),
            pltpu.VMEM((CHUNK, GDN_WIDTH), F32),
        ],
        compiler_params=_params(("arbitrary", "arbitrary")),
        name="gdn_heads",
    )(qkvz, qkvz, qkvz, qkvz, ba, conv_w, conv_w, conv_w, a_log_row, dt_bias_row, norm_w)


HIST = 32


def _cconv_kernel(val_ref, gate_ref, bv_ref, bg_ref, dww_ref, dwb_ref, lnw_ref, lnb_ref, o_ref,
                  pad, sh, conv, *, tc, rt):
    t = pl.program_id(1)
    lead = HIST - (DW_KERNEL - 1)
    nsh = tc + HIST - SUBLANES
    ncs = CONV_WIDTH // LANES

    @pl.when(t == 0)
    def _():
        pad[0:HIST, :] = jnp.zeros((HIST, CONV_WIDTH), F32)

    pad[HIST:HIST + tc, :] = (val_ref[...] + bv_ref[...]) * jax.nn.sigmoid(gate_ref[...] + bg_ref[...])

    def stripe(cs, carry):
        c0 = pl.multiple_of(cs * LANES, LANES)
        for r in range(1, SUBLANES):
            sh[r - 1] = pad[r:r + nsh, pl.ds(c0, LANES)]
        for rr in range(tc // rt):
            acc = jnp.broadcast_to(dwb_ref[:, pl.ds(c0, LANES)], (rt, LANES))
            for j in range(DW_KERNEL):
                r = (lead + j) % SUBLANES
                row = rr * rt + lead + j - r
                if r == 0:
                    xs = pad[row:row + rt, pl.ds(c0, LANES)]
                else:
                    xs = sh[r - 1, row:row + rt, :]
                acc = acc + xs * dww_ref[j:j + 1, pl.ds(c0, LANES)]
            conv[rr * rt:(rr + 1) * rt, pl.ds(c0, LANES)] = acc
        return carry

    lax.fori_loop(0, ncs, stripe, 0)
    pad[0:HIST, :] = pad[tc:tc + HIST, :]

    ln_unroll = 4

    def rows(r, carry):
        r0s = [pl.multiple_of((r * ln_unroll + u) * rt, rt) for u in range(ln_unroll)]
        ys = [conv[pl.ds(r0, rt), :] for r0 in r0s]
        mus = [jnp.mean(y, axis=-1, keepdims=True) for y in ys]
        ycs = [y - mu for y, mu in zip(ys, mus)]
        vrs = [jnp.mean(yc * yc, axis=-1, keepdims=True) for yc in ycs]
        for r0, yc, var in zip(r0s, ycs, vrs):
            yn = yc * lax.rsqrt(var + LN_EPS) * lnw_ref[...] + lnb_ref[...]
            o_ref[pl.ds(r0, rt), :] = (yn * jax.nn.sigmoid(yn)).astype(o_ref.dtype)
        return carry

    lax.fori_loop(0, tc // (rt * ln_unroll), rows, 0)


def conformer_conv(glu, pw_b, dw_w, dw_b, ln_w, ln_b, bsz, t_len, tc=256, rt=32):
    m = bsz * t_len
    nt = t_len // tc
    kern = functools.partial(_cconv_kernel, tc=tc, rt=rt)
    half = lambda c: pl.BlockSpec((tc, CONV_WIDTH), lambda b, t: (b * nt + t, c))
    bias = lambda c: pl.BlockSpec((1, CONV_WIDTH), lambda b, t: (0, c))
    vec = pl.BlockSpec((1, CONV_WIDTH), lambda b, t: (0, 0))
    return pl.pallas_call(
        kern,
        grid=(bsz, nt),
        in_specs=[half(0), half(1), bias(0), bias(1),
                  pl.BlockSpec((DW_KERNEL, CONV_WIDTH), lambda b, t: (0, 0)), vec, vec, vec],
        out_specs=pl.BlockSpec((tc, CONV_WIDTH), lambda b, t: (b * nt + t, 0)),
        out_shape=jax.ShapeDtypeStruct((m, CONV_WIDTH), BF16),
        scratch_shapes=[pltpu.VMEM((tc + HIST, CONV_WIDTH), F32),
                        pltpu.VMEM((SUBLANES - 1, tc + HIST - SUBLANES, LANES), F32),
                        pltpu.VMEM((tc, CONV_WIDTH), F32)],
        compiler_params=_params(("arbitrary", "arbitrary")),
        name="conformer_conv",
    )(glu, glu, pw_b.reshape(1, -1), pw_b.reshape(1, -1), dw_w,
      dw_b.reshape(1, -1), ln_w.reshape(1, -1), ln_b.reshape(1, -1))


def _pad_lanes(v, offset):
    out = jnp.zeros((1, LANES), F32)
    return lax.dynamic_update_slice(out, v.reshape(1, -1).astype(F32), (0, offset))


def kernel(x, pre_mix_norm, w_in, gdn_conv_w, gdn_a_log, gdn_dt_bias, gdn_norm_w, cm_pw_b, cm_dw_w,
           cm_dw_b, cm_ln_w, cm_ln_b, w_out, post_mix_norm, pre_ffn_norm, w_gate, w_up, w_down,
           post_ffn_norm):
    bsz, t_len, d = x.shape
    depth = w_in.shape[0]
    m = bsz * t_len
    xf = x.reshape(m, d)
    o_z = QKV_WIDTH + GDN_WIDTH
    o_g = o_z + 2 * GDN_HEADS

    w_in_t = jnp.swapaxes(w_in, 1, 2)
    w_ba_t = jnp.pad(w_in_t[:, o_z:o_g, :], ((0, 0), (0, LANES - 2 * GDN_HEADS), (0, 0)))
    w_glu_t = w_in_t[:, o_g:, :].astype(BF16)

    h = rmsnorm_bf16(xf, pre_mix_norm[0])
    for l in range(depth):
        qkvz = matmul_nt(h, w_in_t, l, o_z, 2048, 512, name="in_proj_qkvz")
        ba = matmul_nt(h, w_ba_t, l, LANES, 2048, LANES, name="in_proj_gates")
        glu = matmul_nt(h, w_glu_t, l, 2 * CONV_WIDTH, 2048, 512, name="in_proj_glu")

        o_a = gdn_heads(qkvz, ba, gdn_conv_w[l], _pad_lanes(gdn_a_log[l], GDN_HEADS),
                        _pad_lanes(gdn_dt_bias[l], GDN_HEADS), gdn_norm_w[l].reshape(1, HEAD_DIM),
                        bsz, t_len)
        c = conformer_conv(glu, cm_pw_b[l], cm_dw_w[l], cm_dw_b[l], cm_ln_w[l], cm_ln_b[l], bsz, t_len)

        mix = matmul_cat(o_a, c, w_out, l, 2048, 512, name="out_proj")
        xf, hf = residual_norm(mix, xf, post_mix_norm[l], pre_ffn_norm[l])

        act = ffn_up(hf, w_gate, w_up, l, 2048, 256)
        ff = matmul(act, w_down, l, 1024, 256, name="ffn_down")
        if l + 1 < depth:
            xf, h = residual_norm(ff, xf, post_ffn_norm[l], pre_mix_norm[l + 1])
        else:
            xf = residual_norm(ff, xf, post_ffn_norm[l])
    return xf.reshape(bsz, t_len, d)
```

```python
import functools

import jax
import jax.numpy as jnp
from jax import lax
from jax.experimental import pallas as pl
from jax.experimental.pallas import tpu as pltpu

D_MODEL = 4096
CHUNK = 64
HEAD_DIM = 128
GDN_WIDTH = D_MODEL // 2
GDN_HEADS = GDN_WIDTH // HEAD_DIM
QKV_WIDTH = 3 * GDN_WIDTH
SHORT_CONV = 4
CONV_WIDTH = D_MODEL - GDN_WIDTH
DW_KERNEL = 31
RMS_EPS = 1e-6
LN_EPS = 1e-5
L2_EPS = 1e-6

LANES = 128
SUBLANES = 8
MXU_WIDTH = 256
VMEM_LIMIT = 56 * 1024 * 1024

F32 = jnp.float32
BF16 = jnp.bfloat16


def _params(sem):
    return pltpu.CompilerParams(dimension_semantics=sem, vmem_limit_bytes=VMEM_LIMIT)


def _sigmoid(x):
    return 0.5 * jnp.tanh(0.5 * x) + 0.5


def _silu(x):
    hx = 0.5 * x
    return hx * jnp.tanh(hx) + hx


def _rms_kernel(x_ref, w_ref, h_ref):
    x = x_ref[...]
    r = lax.rsqrt(jnp.mean(x * x, axis=-1, keepdims=True) + RMS_EPS)
    h_ref[...] = (x * r * w_ref[...]).astype(h_ref.dtype)


def rmsnorm_bf16(x, w, tr=256):
    m, d = x.shape
    return pl.pallas_call(
        _rms_kernel,
        grid=(m // tr,),
        in_specs=[pl.BlockSpec((tr, d), lambda i: (i, 0)),
                  pl.BlockSpec((1, d), lambda i: (0, 0))],
        out_specs=pl.BlockSpec((tr, d), lambda i: (i, 0)),
        out_shape=jax.ShapeDtypeStruct((m, d), BF16),
        compiler_params=_params(("arbitrary",)),
        name="rmsnorm",
    )(x, w.reshape(1, d))


def _row_block(tm, k, buffers=2):
    return pl.BlockSpec((tm, k), lambda i, j: (i, 0), pipeline_mode=pl.Buffered(buffers))


def _dot_nt(a, b):
    return lax.dot_general(a, b, (((1,), (1,)), ((), ())), preferred_element_type=F32)


def _dot_tn(a, b):
    return lax.dot_general(a, b, (((0,), (0,)), ((), ())), preferred_element_type=F32)


def _mm_kernel(a_ref, w_ref, o_ref):
    o_ref[...] = jnp.dot(a_ref[...], w_ref[...].astype(BF16), preferred_element_type=F32)


def _mm_nt_kernel(a_ref, wt_ref, o_ref):
    o_ref[...] = _dot_nt(a_ref[...], wt_ref[...].astype(BF16))


def matmul(a, w_stack, layer, tm, tn, name):
    m, k = a.shape
    n = w_stack.shape[-1]
    return pl.pallas_call(
        _mm_kernel,
        grid=(m // tm, n // tn),
        in_specs=[_row_block(tm, k, buffers=1),
                  pl.BlockSpec((None, k, tn), lambda i, j: (layer, 0, j))],
        out_specs=pl.BlockSpec((tm, tn), lambda i, j: (i, j)),
        out_shape=jax.ShapeDtypeStruct((m, n), F32),
        compiler_params=_params(("arbitrary", "arbitrary")),
        name=name,
    )(a, w_stack)


def matmul_nt(a, wt_stack, layer, n, tm, tn, name, row_buffers=2):
    m, k = a.shape
    return pl.pallas_call(
        _mm_nt_kernel,
        grid=(m // tm, n // tn),
        in_specs=[_row_block(tm, k, row_buffers),
                  pl.BlockSpec((None, tn, k), lambda i, j: (layer, j, 0))],
        out_specs=pl.BlockSpec((tm, tn), lambda i, j: (i, j)),
        out_shape=jax.ShapeDtypeStruct((m, n), F32),
        compiler_params=_params(("arbitrary", "arbitrary")),
        name=name,
    )(a, wt_stack)


def _mm_cat_kernel(a1_ref, a2_ref, w1_ref, w2_ref, o_ref):
    acc = jnp.dot(a1_ref[...], w1_ref[...].astype(BF16), preferred_element_type=F32)
    acc += jnp.dot(a2_ref[...], w2_ref[...].astype(BF16), preferred_element_type=F32)
    o_ref[...] = acc


def matmul_cat(a1, a2, w_stack, layer, tm, tn, name):
    m, k1 = a1.shape
    _, k2 = a2.shape
    n = w_stack.shape[-1]
    assert k1 == k2
    return pl.pallas_call(
        _mm_cat_kernel,
        grid=(m // tm, n // tn),
        in_specs=[_row_block(tm, k1, buffers=1), _row_block(tm, k2, buffers=1),
                  pl.BlockSpec((None, k1, tn), lambda i, j: (layer, 0, j)),
                  pl.BlockSpec((None, k2, tn), lambda i, j: (layer, 1, j))],
        out_specs=pl.BlockSpec((tm, tn), lambda i, j: (i, j)),
        out_shape=jax.ShapeDtypeStruct((m, n), F32),
        compiler_params=_params(("arbitrary", "arbitrary")),
        name=name,
    )(a1, a2, w_stack, w_stack)


def _ffn_up_kernel(h_ref, wg_ref, wu_ref, o_ref):
    h = h_ref[...]
    g = jnp.dot(h, wg_ref[...].astype(BF16), preferred_element_type=F32)
    u = jnp.dot(h, wu_ref[...].astype(BF16), preferred_element_type=F32)
    o_ref[...] = (_silu(g) * u).astype(o_ref.dtype)


def ffn_up(h, wg_stack, wu_stack, layer, tm, tn):
    m, k = h.shape
    n = wg_stack.shape[-1]
    wspec = pl.BlockSpec((None, k, tn), lambda i, j: (layer, 0, j))
    return pl.pallas_call(
        _ffn_up_kernel,
        grid=(m // tm, n // tn),
        in_specs=[_row_block(tm, k, buffers=1), wspec, wspec],
        out_specs=pl.BlockSpec((tm, tn), lambda i, j: (i, j)),
        out_shape=jax.ShapeDtypeStruct((m, n), BF16),
        compiler_params=_params(("arbitrary", "arbitrary")),
        name="ffn_up",
    )(h, wg_stack, wu_stack)


def _post_kernel(y_ref, x_ref, wp_ref, wn_ref, xo_ref, h_ref):
    y = y_ref[...]
    r = lax.rsqrt(jnp.mean(y * y, axis=-1, keepdims=True) + RMS_EPS)
    xn = x_ref[...] + y * r * wp_ref[...]
    xo_ref[...] = xn
    r2 = lax.rsqrt(jnp.mean(xn * xn, axis=-1, keepdims=True) + RMS_EPS)
    h_ref[...] = (xn * r2 * wn_ref[...]).astype(h_ref.dtype)


def _post_last_kernel(y_ref, x_ref, wp_ref, xo_ref):
    y = y_ref[...]
    r = lax.rsqrt(jnp.mean(y * y, axis=-1, keepdims=True) + RMS_EPS)
    xo_ref[...] = x_ref[...] + y * r * wp_ref[...]


def residual_norm(y, x, w_post, w_next=None, tr=256):
    m, d = x.shape
    row = pl.BlockSpec((tr, d), lambda i: (i, 0))
    vec = pl.BlockSpec((1, d), lambda i: (0, 0))
    if w_next is None:
        return pl.pallas_call(
            _post_last_kernel, grid=(m // tr,),
            in_specs=[row, row, vec], out_specs=row,
            out_shape=jax.ShapeDtypeStruct((m, d), F32),
            compiler_params=_params(("arbitrary",)), name="residual_norm_last",
        )(y, x, w_post.reshape(1, d))
    return pl.pallas_call(
        _post_kernel, grid=(m // tr,),
        in_specs=[row, row, vec, vec], out_specs=[row, row],
        out_shape=[jax.ShapeDtypeStruct((m, d), F32), jax.ShapeDtypeStruct((m, d), BF16)],
        compiler_params=_params(("arbitrary",)), name="residual_norm",
    )(y, x, w_post.reshape(1, d), w_next.reshape(1, d))


def _dot_f32(a, b):
    return jnp.dot(a, b, precision=lax.Precision.HIGHEST, preferred_element_type=F32)


def _gdn_kernel(q_ref, k_ref, v_ref, z_ref, ba_ref, cwq_ref, cwk_ref, cwv_ref,
                alog_ref, dtb_ref, nw_ref, o_ref,
                s_ref, qpad, kpad, vpad, qa, ka, va):
    t = pl.program_id(1)
    pad0 = SUBLANES

    @pl.when(t == 0)
    def _():
        s_ref[...] = jnp.zeros_like(s_ref)
        qpad[0:pad0, :] = jnp.zeros((pad0, GDN_WIDTH), F32)
        kpad[0:pad0, :] = jnp.zeros((pad0, GDN_WIDTH), F32)
        vpad[0:pad0, :] = jnp.zeros((pad0, GDN_WIDTH), F32)

    def conv_silu(pad, x_ref, cw_ref, dst):
        pad[pad0:pad0 + CHUNK, :] = x_ref[...]
        acc = None
        for j in range(SHORT_CONV):
            off = pad0 - (SHORT_CONV - 1) + j
            term = pad[off:off + CHUNK, :] * cw_ref[j:j + 1, :]
            acc = term if acc is None else acc + term
        dst[...] = _silu(acc)
        pad[0:pad0, :] = pad[CHUNK:CHUNK + pad0, :]

    conv_silu(qpad, q_ref, cwq_ref, qa)
    conv_silu(kpad, k_ref, cwk_ref, ka)
    conv_silu(vpad, v_ref, cwv_ref, va)

    ii = lax.broadcasted_iota(jnp.int32, (CHUNK, CHUNK), 0)
    jj = lax.broadcasted_iota(jnp.int32, (CHUNK, CHUNK), 1)
    causal = jj <= ii
    strict = jj < ii
    eye = jnp.where(ii == jj, 1.0, 0.0).astype(F32)
    level_masks = []
    s_blk = 1
    while s_blk < CHUNK:
        level_masks.append(((ii // (2 * s_blk)) == (jj // (2 * s_blk))) & ((ii // s_blk) != (jj // s_blk)))
        s_blk *= 2
    nw = nw_ref[...]

    def bdot(a, b):
        return jnp.dot(a.astype(BF16), b.astype(BF16), preferred_element_type=F32)

    ba = ba_ref[...]
    beta_all = _sigmoid(ba)
    x = ba + dtb_ref[...]
    softplus = jnp.maximum(x, 0.0) + jnp.log1p(jnp.exp(-jnp.abs(x)))
    g_all = -jnp.exp(alog_ref[...]) * softplus
    gcum = _dot_f32(jnp.where(causal, 1.0, 0.0).astype(F32), g_all)
    gcum_t = gcum.T

    def prep(i):
        cs = slice(i * HEAD_DIM, (i + 1) * HEAD_DIM)
        gcol = gcum[:, GDN_HEADS + i:GDN_HEADS + i + 1]
        grow = gcum_t[GDN_HEADS + i:GDN_HEADS + i + 1, :]
        bcol = beta_all[:, i:i + 1]
        glast = gcol[CHUNK - 1:CHUNK, :]
        q = qa[:, cs]
        k = ka[:, cs]
        q = q * (lax.rsqrt(jnp.sum(q * q, axis=-1, keepdims=True) + L2_EPS) * (HEAD_DIM ** -0.5))
        k = k * lax.rsqrt(jnp.sum(k * k, axis=-1, keepdims=True) + L2_EPS)
        eg = jnp.exp(gcol)
        kbeta = k * bcol
        return dict(
            decay=jnp.exp(jnp.where(causal, gcol - grow, -jnp.inf)),
            k16=k.astype(BF16), qkb=jnp.concatenate([q, kbeta], axis=0).astype(BF16),
            rhs=jnp.concatenate([va[:, cs] * bcol, kbeta * eg], axis=1).astype(BF16),
            qg=q * eg, kd=(k * jnp.exp(glast - gcol)).astype(BF16), dlast=jnp.exp(glast))

    pr = [prep(i) for i in range(GDN_HEADS)]
    qk_kk = [_dot_nt(p["qkb"], p["k16"]) for p in pr]
    a_intra = [(x[:CHUNK] * p["decay"]).astype(BF16) for x, p in zip(qk_kk, pr)]
    lower = [jnp.where(strict, x[CHUNK:] * p["decay"], 0.0) for x, p in zip(qk_kk, pr)]

    t_inv = [eye - jnp.where(level_masks[0], lo, 0.0) for lo in lower]
    for mask in level_masks[1:]:
        y = [bdot(jnp.where(mask, lo, 0.0), t) for lo, t in zip(lower, t_inv)]
        t_inv = [t - bdot(t, yy) for t, yy in zip(t_inv, y)]
    uw = [jnp.dot(t.astype(BF16), p["rhs"], preferred_element_type=F32) for t, p in zip(t_inv, pr)]
    wq = [jnp.concatenate([x[:, HEAD_DIM:], p["qg"]], axis=0).astype(BF16) for x, p in zip(uw, pr)]

    ws_qs = [jnp.dot(wq[i], s_ref[i].astype(BF16), preferred_element_type=F32)
             for i in range(GDN_HEADS)]
    vn16 = [(uw[i][:, :HEAD_DIM] - ws_qs[i][:CHUNK]).astype(BF16) for i in range(GDN_HEADS)]
    av = [jnp.dot(a_intra[i], vn16[i], preferred_element_type=F32) for i in range(GDN_HEADS)]
    kv = [_dot_tn(pr[i]["kd"], vn16[i]) for i in range(GDN_HEADS)]
    for i in range(GDN_HEADS):
        cs = slice(i * HEAD_DIM, (i + 1) * HEAD_DIM)
        s_ref[i] = s_ref[i] * pr[i]["dlast"] + kv[i]
        o = ws_qs[i][CHUNK:] + av[i]
        o = o * lax.rsqrt(jnp.mean(o * o, axis=-1, keepdims=True) + RMS_EPS) * nw
        o_ref[:, cs] = (o * _silu(z_ref[:, cs])).astype(o_ref.dtype)


def gdn_heads(proj, conv_w, a_log_row, dt_bias_row, norm_w, bsz, t_len):
    m = bsz * t_len
    nt = t_len // CHUNK
    col = lambda off: pl.BlockSpec((CHUNK, GDN_WIDTH), lambda b, t: (b * nt + t, off))
    cw = lambda off: pl.BlockSpec((SHORT_CONV, GDN_WIDTH), lambda b, t: (0, off))
    vec = pl.BlockSpec((1, LANES), lambda b, t: (0, 0))
    gate_block = (QKV_WIDTH + GDN_WIDTH) // LANES
    return pl.pallas_call(
        _gdn_kernel,
        grid=(bsz, nt),
        in_specs=[col(0), col(1), col(2), col(3),
                  pl.BlockSpec((CHUNK, LANES), lambda b, t: (b * nt + t, gate_block)),
                  cw(0), cw(1), cw(2), vec, vec, vec],
        out_specs=pl.BlockSpec((CHUNK, GDN_WIDTH), lambda b, t: (b * nt + t, 0)),
        out_shape=jax.ShapeDtypeStruct((m, GDN_WIDTH), BF16),
        scratch_shapes=[
            pltpu.VMEM((GDN_HEADS, HEAD_DIM, HEAD_DIM), F32),
            pltpu.VMEM((CHUNK + SUBLANES, GDN_WIDTH), F32),
            pltpu.VMEM((CHUNK + SUBLANES, GDN_WIDTH), F32),
            pltpu.VMEM((CHUNK + SUBLANES, GDN_WIDTH), F32),
            pltpu.VMEM((CHUNK, GDN_WIDTH), F32),
            pltpu.VMEM((CHUNK, GDN_WIDTH), F32),
            pltpu.VMEM((CHUNK, GDN_WIDTH), F32),
        ],
        compiler_params=_params(("arbitrary", "arbitrary")),
        name="gdn_heads",
    )(proj, proj, proj, proj, proj, conv_w, conv_w, conv_w, a_log_row, dt_bias_row, norm_w)


HIST = 32


def _cconv_kernel(val_ref, gate_ref, bv_ref, bg_ref, dww_ref, dwb_ref, lnw_ref, lnb_ref, o_ref,
                  pad, sh, conv, *, tc, rt):
    t = pl.program_id(1)
    lead = HIST - (DW_KERNEL - 1)
    nsh = tc + HIST - SUBLANES
    ncs = CONV_WIDTH // LANES

    @pl.when(t == 0)
    def _():
        pad[0:HIST, :] = jnp.zeros((HIST, CONV_WIDTH), F32)

    pad[HIST:HIST + tc, :] = (val_ref[...] + bv_ref[...]) * _sigmoid(gate_ref[...] + bg_ref[...])

    def stripe(cs, carry):
        c0 = pl.multiple_of(cs * LANES, LANES)
        for r in range(1, SUBLANES):
            sh[r - 1] = pad[r:r + nsh, pl.ds(c0, LANES)]
        for rr in range(tc // rt):
            acc = jnp.broadcast_to(dwb_ref[:, pl.ds(c0, LANES)], (rt, LANES))
            for j in range(DW_KERNEL):
                r = (lead + j) % SUBLANES
                row = rr * rt + lead + j - r
                if r == 0:
                    xs = pad[row:row + rt, pl.ds(c0, LANES)]
                else:
                    xs = sh[r - 1, row:row + rt, :]
                acc = acc + xs * dww_ref[j:j + 1, pl.ds(c0, LANES)]
            conv[rr * rt:(rr + 1) * rt, pl.ds(c0, LANES)] = acc
        return carry

    lax.fori_loop(0, ncs, stripe, 0)
    pad[0:HIST, :] = pad[tc:tc + HIST, :]

    ln_unroll = 4

    def rows(r, carry):
        r0s = [pl.multiple_of((r * ln_unroll + u) * rt, rt) for u in range(ln_unroll)]
        ys = [conv[pl.ds(r0, rt), :] for r0 in r0s]
        mus = [jnp.mean(y, axis=-1, keepdims=True) for y in ys]
        ycs = [y - mu for y, mu in zip(ys, mus)]
        vrs = [jnp.mean(yc * yc, axis=-1, keepdims=True) for yc in ycs]
        for r0, yc, var in zip(r0s, ycs, vrs):
            yn = yc * lax.rsqrt(var + LN_EPS) * lnw_ref[...] + lnb_ref[...]
            o_ref[pl.ds(r0, rt), :] = _silu(yn).astype(o_ref.dtype)
        return carry

    lax.fori_loop(0, tc // (rt * ln_unroll), rows, 0)


def conformer_conv(glu, pw_b, dw_w, dw_b, ln_w, ln_b, bsz, t_len, tc=256, rt=32):
    m = bsz * t_len
    nt = t_len // tc
    kern = functools.partial(_cconv_kernel, tc=tc, rt=rt)
    half = lambda c: pl.BlockSpec((tc, CONV_WIDTH), lambda b, t: (b * nt + t, c))
    bias = lambda c: pl.BlockSpec((1, CONV_WIDTH), lambda b, t: (0, c))
    vec = pl.BlockSpec((1, CONV_WIDTH), lambda b, t: (0, 0))
    return pl.pallas_call(
        kern,
        grid=(bsz, nt),
        in_specs=[half(0), half(1), bias(0), bias(1),
                  pl.BlockSpec((DW_KERNEL, CONV_WIDTH), lambda b, t: (0, 0)), vec, vec, vec],
        out_specs=pl.BlockSpec((tc, CONV_WIDTH), lambda b, t: (b * nt + t, 0)),
        out_shape=jax.ShapeDtypeStruct((m, CONV_WIDTH), BF16),
        scratch_shapes=[pltpu.VMEM((tc + HIST, CONV_WIDTH), F32),
                        pltpu.VMEM((SUBLANES - 1, tc + HIST - SUBLANES, LANES), F32),
                        pltpu.VMEM((tc, CONV_WIDTH), F32)],
        compiler_params=_params(("arbitrary", "arbitrary")),
        name="conformer_conv",
    )(glu, glu, pw_b.reshape(1, -1), pw_b.reshape(1, -1), dw_w,
      dw_b.reshape(1, -1), ln_w.reshape(1, -1), ln_b.reshape(1, -1))


def _pad_lanes(v, offset):
    out = jnp.zeros((1, LANES), F32)
    return lax.dynamic_update_slice(out, v.reshape(1, -1).astype(F32), (0, offset))


def kernel(x, pre_mix_norm, w_in, gdn_conv_w, gdn_a_log, gdn_dt_bias, gdn_norm_w, cm_pw_b, cm_dw_w,
           cm_dw_b, cm_ln_w, cm_ln_b, w_out, post_mix_norm, pre_ffn_norm, w_gate, w_up, w_down,
           post_ffn_norm):
    bsz, t_len, d = x.shape
    depth = w_in.shape[0]
    m = bsz * t_len
    xf = x.reshape(m, d)
    o_z = QKV_WIDTH + GDN_WIDTH
    o_g = o_z + 2 * GDN_HEADS
    tn_in = 2 * MXU_WIDTH
    n_gdn = o_z + tn_in

    w_in_t = jnp.swapaxes(w_in, 1, 2)
    w_glu_t = w_in_t[:, o_g:, :].astype(BF16)

    h = rmsnorm_bf16(xf, pre_mix_norm[0])
    for l in range(depth):
        proj = matmul_nt(h, w_in_t, l, n_gdn, 2048, tn_in, name="in_proj_gdn", row_buffers=1)
        glu = matmul_nt(h, w_glu_t, l, 2 * CONV_WIDTH, 2048, tn_in, name="in_proj_glu")

        o_a = gdn_heads(proj, gdn_conv_w[l], _pad_lanes(gdn_a_log[l], GDN_HEADS),
                        _pad_lanes(gdn_dt_bias[l], GDN_HEADS), gdn_norm_w[l].reshape(1, HEAD_DIM),
                        bsz, t_len)
        c = conformer_conv(glu, cm_pw_b[l], cm_dw_w[l], cm_dw_b[l], cm_ln_w[l], cm_ln_b[l], bsz, t_len)

        mix = matmul_cat(o_a, c, w_out, l, 2048, tn_in, name="out_proj")
        xf, hf = residual_norm(mix, xf, post_mix_norm[l], pre_ffn_norm[l])

        act = ffn_up(hf, w_gate, w_up, l, 2048, MXU_WIDTH)
        ff = matmul(act, w_down, l, 1024, MXU_WIDTH, name="ffn_down")
        if l + 1 < depth:
            xf, h = residual_norm(ff, xf, post_ffn_norm[l], pre_mix_norm[l + 1])
        else:
            xf = residual_norm(ff, xf, post_ffn_norm[l])
    return xf.reshape(bsz, t_len, d)
```

```python
import functools

import jax
import jax.numpy as jnp
from jax import lax
from jax.experimental import pallas as pl
from jax.experimental.pallas import tpu as pltpu

D_MODEL = 4096
CHUNK = 64
HEAD_DIM = 128
GDN_WIDTH = D_MODEL // 2
GDN_HEADS = GDN_WIDTH // HEAD_DIM
QKV_WIDTH = 3 * GDN_WIDTH
SHORT_CONV = 4
CONV_WIDTH = D_MODEL - GDN_WIDTH
DW_KERNEL = 31
RMS_EPS = 1e-6
LN_EPS = 1e-5
L2_EPS = 1e-6

LANES = 128
SUBLANES = 8
MXU_WIDTH = 256
VMEM_LIMIT = 56 * 1024 * 1024

F32 = jnp.float32
BF16 = jnp.bfloat16


def _params(sem):
    return pltpu.CompilerParams(dimension_semantics=sem, vmem_limit_bytes=VMEM_LIMIT)


def _sigmoid(x):
    return 0.5 * jnp.tanh(0.5 * x) + 0.5


def _silu(x):
    hx = 0.5 * x
    return hx * jnp.tanh(hx) + hx


def _rms_kernel(x_ref, w_ref, h_ref):
    x = x_ref[...]
    r = lax.rsqrt(jnp.mean(x * x, axis=-1, keepdims=True) + RMS_EPS)
    h_ref[...] = (x * r * w_ref[...]).astype(h_ref.dtype)


def rmsnorm_bf16(x, w, tr=512):
    m, d = x.shape
    return pl.pallas_call(
        _rms_kernel,
        grid=(m // tr,),
        in_specs=[pl.BlockSpec((tr, d), lambda i: (i, 0)),
                  pl.BlockSpec((1, d), lambda i: (0, 0))],
        out_specs=pl.BlockSpec((tr, d), lambda i: (i, 0)),
        out_shape=jax.ShapeDtypeStruct((m, d), BF16),
        compiler_params=_params(("arbitrary",)),
        name="rmsnorm",
    )(x, w.reshape(1, d))


def _row_block(tm, k, buffers=2):
    return pl.BlockSpec((tm, k), lambda i, j: (i, 0), pipeline_mode=pl.Buffered(buffers))


def _dot_nt(a, b):
    return lax.dot_general(a, b, (((1,), (1,)), ((), ())), preferred_element_type=F32)


def _dot_tn(a, b):
    return lax.dot_general(a, b, (((0,), (0,)), ((), ())), preferred_element_type=F32)


def _mm_kernel(a_ref, w_ref, o_ref):
    o_ref[...] = jnp.dot(a_ref[...], w_ref[...].astype(BF16),
                         preferred_element_type=F32).astype(o_ref.dtype)


def _mm_nt_kernel(a_ref, wt_ref, o_ref):
    o_ref[...] = _dot_nt(a_ref[...], wt_ref[...].astype(BF16))


def matmul(a, w_stack, layer, tm, tn, out_dtype, name):
    m, k = a.shape
    n = w_stack.shape[-1]
    return pl.pallas_call(
        _mm_kernel,
        grid=(m // tm, n // tn),
        in_specs=[_row_block(tm, k, buffers=1),
                  pl.BlockSpec((None, k, tn), lambda i, j: (layer, 0, j))],
        out_specs=pl.BlockSpec((tm, tn), lambda i, j: (i, j)),
        out_shape=jax.ShapeDtypeStruct((m, n), out_dtype),
        compiler_params=_params(("arbitrary", "arbitrary")),
        name=name,
    )(a, w_stack)


def matmul_nt(a, wt_stack, layer, n, tm, tn, name, row_buffers=2):
    m, k = a.shape
    return pl.pallas_call(
        _mm_nt_kernel,
        grid=(m // tm, n // tn),
        in_specs=[_row_block(tm, k, row_buffers),
                  pl.BlockSpec((None, tn, k), lambda i, j: (layer, j, 0))],
        out_specs=pl.BlockSpec((tm, tn), lambda i, j: (i, j)),
        out_shape=jax.ShapeDtypeStruct((m, n), F32),
        compiler_params=_params(("arbitrary", "arbitrary")),
        name=name,
    )(a, wt_stack)


def _mm_cat_kernel(a1_ref, a2_ref, w1_ref, w2_ref, o_ref):
    acc = jnp.dot(a1_ref[...], w1_ref[...].astype(BF16), preferred_element_type=F32)
    acc += jnp.dot(a2_ref[...], w2_ref[...].astype(BF16), preferred_element_type=F32)
    o_ref[...] = acc.astype(o_ref.dtype)


def matmul_cat(a1, a2, w_stack, layer, tm, tn, out_dtype, name):
    m, k1 = a1.shape
    _, k2 = a2.shape
    n = w_stack.shape[-1]
    assert k1 == k2
    return pl.pallas_call(
        _mm_cat_kernel,
        grid=(m // tm, n // tn),
        in_specs=[_row_block(tm, k1, buffers=1), _row_block(tm, k2, buffers=1),
                  pl.BlockSpec((None, k1, tn), lambda i, j: (layer, 0, j)),
                  pl.BlockSpec((None, k2, tn), lambda i, j: (layer, 1, j))],
        out_specs=pl.BlockSpec((tm, tn), lambda i, j: (i, j)),
        out_shape=jax.ShapeDtypeStruct((m, n), out_dtype),
        compiler_params=_params(("arbitrary", "arbitrary")),
        name=name,
    )(a1, a2, w_stack, w_stack)


def _ffn_up_kernel(h_ref, wg_ref, wu_ref, o_ref):
    h = h_ref[...]
    g = jnp.dot(h, wg_ref[...].astype(BF16), preferred_element_type=F32)
    u = jnp.dot(h, wu_ref[...].astype(BF16), preferred_element_type=F32)
    o_ref[...] = (_silu(g) * u).astype(o_ref.dtype)


def ffn_up(h, wg_stack, wu_stack, layer, tm, tn):
    m, k = h.shape
    n = wg_stack.shape[-1]
    wspec = pl.BlockSpec((None, k, tn), lambda i, j: (layer, 0, j))
    return pl.pallas_call(
        _ffn_up_kernel,
        grid=(m // tm, n // tn),
        in_specs=[_row_block(tm, k, buffers=1), wspec, wspec],
        out_specs=pl.BlockSpec((tm, tn), lambda i, j: (i, j)),
        out_shape=jax.ShapeDtypeStruct((m, n), BF16),
        compiler_params=_params(("arbitrary", "arbitrary")),
        name="ffn_up",
    )(h, wg_stack, wu_stack)


def _post_kernel(y_ref, x_ref, wp_ref, wn_ref, xo_ref, h_ref):
    y = y_ref[...].astype(F32)
    r = lax.rsqrt(jnp.mean(y * y, axis=-1, keepdims=True) + RMS_EPS)
    xn = x_ref[...] + y * r * wp_ref[...]
    xo_ref[...] = xn
    r2 = lax.rsqrt(jnp.mean(xn * xn, axis=-1, keepdims=True) + RMS_EPS)
    h_ref[...] = (xn * r2 * wn_ref[...]).astype(h_ref.dtype)


def _post_last_kernel(y_ref, x_ref, wp_ref, xo_ref):
    y = y_ref[...].astype(F32)
    r = lax.rsqrt(jnp.mean(y * y, axis=-1, keepdims=True) + RMS_EPS)
    xo_ref[...] = x_ref[...] + y * r * wp_ref[...]


def residual_norm(y, x, w_post, w_next=None, tr=256):
    m, d = x.shape
    row = pl.BlockSpec((tr, d), lambda i: (i, 0))
    vec = pl.BlockSpec((1, d), lambda i: (0, 0))
    if w_next is None:
        return pl.pallas_call(
            _post_last_kernel, grid=(m // tr,),
            in_specs=[row, row, vec], out_specs=row,
            out_shape=jax.ShapeDtypeStruct((m, d), F32),
            compiler_params=_params(("arbitrary",)), name="residual_norm_last",
        )(y, x, w_post.reshape(1, d))
    return pl.pallas_call(
        _post_kernel, grid=(m // tr,),
        in_specs=[row, row, vec, vec], out_specs=[row, row],
        out_shape=[jax.ShapeDtypeStruct((m, d), F32), jax.ShapeDtypeStruct((m, d), BF16)],
        compiler_params=_params(("arbitrary",)), name="residual_norm",
    )(y, x, w_post.reshape(1, d), w_next.reshape(1, d))


def _dot_f32(a, b):
    return jnp.dot(a, b, precision=lax.Precision.HIGHEST, preferred_element_type=F32)


def _gdn_kernel(q_ref, k_ref, v_ref, z_ref, ba_ref, cwq_ref, cwk_ref, cwv_ref,
                alog_ref, dtb_ref, nw_ref, o_ref,
                s_ref, qpad, kpad, vpad, qa, ka, va):
    t = pl.program_id(1)
    pad0 = SUBLANES

    @pl.when(t == 0)
    def _():
        s_ref[...] = jnp.zeros_like(s_ref)
        qpad[0:pad0, :] = jnp.zeros((pad0, GDN_WIDTH), F32)
        kpad[0:pad0, :] = jnp.zeros((pad0, GDN_WIDTH), F32)
        vpad[0:pad0, :] = jnp.zeros((pad0, GDN_WIDTH), F32)

    def conv_silu(pad, x_ref, cw_ref, dst):
        pad[pad0:pad0 + CHUNK, :] = x_ref[...]
        acc = None
        for j in range(SHORT_CONV):
            off = pad0 - (SHORT_CONV - 1) + j
            term = pad[off:off + CHUNK, :] * cw_ref[j:j + 1, :]
            acc = term if acc is None else acc + term
        dst[...] = _silu(acc)
        pad[0:pad0, :] = pad[CHUNK:CHUNK + pad0, :]

    conv_silu(qpad, q_ref, cwq_ref, qa)
    conv_silu(kpad, k_ref, cwk_ref, ka)
    conv_silu(vpad, v_ref, cwv_ref, va)

    ii = lax.broadcasted_iota(jnp.int32, (CHUNK, CHUNK), 0)
    jj = lax.broadcasted_iota(jnp.int32, (CHUNK, CHUNK), 1)
    causal = jj <= ii
    strict = jj < ii
    eye = jnp.where(ii == jj, 1.0, 0.0).astype(F32)
    level_masks = []
    s_blk = 1
    while s_blk < CHUNK:
        level_masks.append(((ii // (2 * s_blk)) == (jj // (2 * s_blk))) & ((ii // s_blk) != (jj // s_blk)))
        s_blk *= 2
    nw = nw_ref[...]

    def bdot(a, b):
        return jnp.dot(a.astype(BF16), b.astype(BF16), preferred_element_type=F32)

    ba = ba_ref[...]
    beta_all = _sigmoid(ba)
    x = ba + dtb_ref[...]
    softplus = jnp.maximum(x, 0.0) + jnp.log1p(jnp.exp(-jnp.abs(x)))
    g_all = -jnp.exp(alog_ref[...]) * softplus
    gcum = _dot_f32(jnp.where(causal, 1.0, 0.0).astype(F32), g_all)
    gcum_t = gcum.T

    def prep(i):
        cs = slice(i * HEAD_DIM, (i + 1) * HEAD_DIM)
        gcol = gcum[:, GDN_HEADS + i:GDN_HEADS + i + 1]
        grow = gcum_t[GDN_HEADS + i:GDN_HEADS + i + 1, :]
        bcol = beta_all[:, i:i + 1]
        glast = gcol[CHUNK - 1:CHUNK, :]
        q = qa[:, cs]
        k = ka[:, cs]
        q = q * (lax.rsqrt(jnp.sum(q * q, axis=-1, keepdims=True) + L2_EPS) * (HEAD_DIM ** -0.5))
        k = k * lax.rsqrt(jnp.sum(k * k, axis=-1, keepdims=True) + L2_EPS)
        eg = jnp.exp(gcol)
        kbeta = k * bcol
        return dict(
            decay=jnp.exp(jnp.where(causal, gcol - grow, -jnp.inf)),
            k16=k.astype(BF16), qkb=jnp.concatenate([q, kbeta], axis=0).astype(BF16),
            rhs=jnp.concatenate([va[:, cs] * bcol, kbeta * eg], axis=1).astype(BF16),
            qg=q * eg, kd=(k * jnp.exp(glast - gcol)).astype(BF16), dlast=jnp.exp(glast))

    pr = [prep(i) for i in range(GDN_HEADS)]
    qk_kk = [_dot_nt(p["qkb"], p["k16"]) for p in pr]
    a_intra = [(x[:CHUNK] * p["decay"]).astype(BF16) for x, p in zip(qk_kk, pr)]
    lower = [jnp.where(strict, x[CHUNK:] * p["decay"], 0.0) for x, p in zip(qk_kk, pr)]

    t_inv = [eye - jnp.where(level_masks[0], lo, 0.0) for lo in lower]
    for mask in level_masks[1:]:
        y = [bdot(jnp.where(mask, lo, 0.0), t) for lo, t in zip(lower, t_inv)]
        t_inv = [t - bdot(t, yy) for t, yy in zip(t_inv, y)]
    uw = [jnp.dot(t.astype(BF16), p["rhs"], preferred_element_type=F32) for t, p in zip(t_inv, pr)]
    wq = [jnp.concatenate([x[:, HEAD_DIM:], p["qg"]], axis=0).astype(BF16) for x, p in zip(uw, pr)]

    ws_qs = [jnp.dot(wq[i], s_ref[i].astype(BF16), preferred_element_type=F32)
             for i in range(GDN_HEADS)]
    vn16 = [(uw[i][:, :HEAD_DIM] - ws_qs[i][:CHUNK]).astype(BF16) for i in range(GDN_HEADS)]
    av = [jnp.dot(a_intra[i], vn16[i], preferred_element_type=F32) for i in range(GDN_HEADS)]
    kv = [_dot_tn(pr[i]["kd"], vn16[i]) for i in range(GDN_HEADS)]
    for i in range(GDN_HEADS):
        cs = slice(i * HEAD_DIM, (i + 1) * HEAD_DIM)
        s_ref[i] = s_ref[i] * pr[i]["dlast"] + kv[i]
        o = ws_qs[i][CHUNK:] + av[i]
        o = o * lax.rsqrt(jnp.mean(o * o, axis=-1, keepdims=True) + RMS_EPS) * nw
        o_ref[:, cs] = (o * _silu(z_ref[:, cs])).astype(o_ref.dtype)


def gdn_heads(proj, conv_w, a_log_row, dt_bias_row, norm_w, bsz, t_len):
    m = bsz * t_len
    nt = t_len // CHUNK
    col = lambda off: pl.BlockSpec((CHUNK, GDN_WIDTH), lambda b, t: (b * nt + t, off))
    cw = lambda off: pl.BlockSpec((SHORT_CONV, GDN_WIDTH), lambda b, t: (0, off))
    vec = pl.BlockSpec((1, LANES), lambda b, t: (0, 0))
    gate_block = (QKV_WIDTH + GDN_WIDTH) // LANES
    return pl.pallas_call(
        _gdn_kernel,
        grid=(bsz, nt),
        in_specs=[col(0), col(1), col(2), col(3),
                  pl.BlockSpec((CHUNK, LANES), lambda b, t: (b * nt + t, gate_block)),
                  cw(0), cw(1), cw(2), vec, vec, vec],
        out_specs=pl.BlockSpec((CHUNK, GDN_WIDTH), lambda b, t: (b * nt + t, 0)),
        out_shape=jax.ShapeDtypeStruct((m, GDN_WIDTH), BF16),
        scratch_shapes=[
            pltpu.VMEM((GDN_HEADS, HEAD_DIM, HEAD_DIM), F32),
            pltpu.VMEM((CHUNK + SUBLANES, GDN_WIDTH), F32),
            pltpu.VMEM((CHUNK + SUBLANES, GDN_WIDTH), F32),
            pltpu.VMEM((CHUNK + SUBLANES, GDN_WIDTH), F32),
            pltpu.VMEM((CHUNK, GDN_WIDTH), F32),
            pltpu.VMEM((CHUNK, GDN_WIDTH), F32),
            pltpu.VMEM((CHUNK, GDN_WIDTH), F32),
        ],
        compiler_params=_params(("arbitrary", "arbitrary")),
        name="gdn_heads",
    )(proj, proj, proj, proj, proj, conv_w, conv_w, conv_w, a_log_row, dt_bias_row, norm_w)


HIST = 32


def _cconv_kernel(val_ref, gate_ref, bv_ref, bg_ref, dww_ref, dwb_ref, lnw_ref, lnb_ref, o_ref,
                  pad, sh, conv, *, tc, rt):
    t = pl.program_id(1)
    lead = HIST - (DW_KERNEL - 1)
    nsh = tc + HIST - SUBLANES
    ncs = CONV_WIDTH // LANES

    @pl.when(t == 0)
    def _():
        pad[0:HIST, :] = jnp.zeros((HIST, CONV_WIDTH), F32)

    pad[HIST:HIST + tc, :] = (val_ref[...] + bv_ref[...]) * _sigmoid(gate_ref[...] + bg_ref[...])

    def stripe(cs, carry):
        c0 = pl.multiple_of(cs * LANES, LANES)
        for r in range(1, SUBLANES):
            sh[r - 1] = pad[r:r + nsh, pl.ds(c0, LANES)]
        for rr in range(tc // rt):
            acc = jnp.broadcast_to(dwb_ref[:, pl.ds(c0, LANES)], (rt, LANES))
            for j in range(DW_KERNEL):
                r = (lead + j) % SUBLANES
                row = rr * rt + lead + j - r
                if r == 0:
                    xs = pad[row:row + rt, pl.ds(c0, LANES)]
                else:
                    xs = sh[r - 1, row:row + rt, :]
                acc = acc + xs * dww_ref[j:j + 1, pl.ds(c0, LANES)]
            conv[rr * rt:(rr + 1) * rt, pl.ds(c0, LANES)] = acc
        return carry

    lax.fori_loop(0, ncs, stripe, 0)
    pad[0:HIST, :] = pad[tc:tc + HIST, :]

    ln_unroll = 4

    def rows(r, carry):
        r0s = [pl.multiple_of((r * ln_unroll + u) * rt, rt) for u in range(ln_unroll)]
        ys = [conv[pl.ds(r0, rt), :] for r0 in r0s]
        mus = [jnp.mean(y, axis=-1, keepdims=True) for y in ys]
        ycs = [y - mu for y, mu in zip(ys, mus)]
        vrs = [jnp.mean(yc * yc, axis=-1, keepdims=True) for yc in ycs]
        for r0, yc, var in zip(r0s, ycs, vrs):
            yn = yc * lax.rsqrt(var + LN_EPS) * lnw_ref[...] + lnb_ref[...]
            o_ref[pl.ds(r0, rt), :] = _silu(yn).astype(o_ref.dtype)
        return carry

    lax.fori_loop(0, tc // (rt * ln_unroll), rows, 0)


def conformer_conv(glu, pw_b, dw_w, dw_b, ln_w, ln_b, bsz, t_len, tc=256, rt=32):
    m = bsz * t_len
    nt = t_len // tc
    kern = functools.partial(_cconv_kernel, tc=tc, rt=rt)
    half = lambda c: pl.BlockSpec((tc, CONV_WIDTH), lambda b, t: (b * nt + t, c))
    bias = lambda c: pl.BlockSpec((1, CONV_WIDTH), lambda b, t: (0, c))
    vec = pl.BlockSpec((1, CONV_WIDTH), lambda b, t: (0, 0))
    return pl.pallas_call(
        kern,
        grid=(bsz, nt),
        in_specs=[half(0), half(1), bias(0), bias(1),
                  pl.BlockSpec((DW_KERNEL, CONV_WIDTH), lambda b, t: (0, 0)), vec, vec, vec],
        out_specs=pl.BlockSpec((tc, CONV_WIDTH), lambda b, t: (b * nt + t, 0)),
        out_shape=jax.ShapeDtypeStruct((m, CONV_WIDTH), BF16),
        scratch_shapes=[pltpu.VMEM((tc + HIST, CONV_WIDTH), F32),
                        pltpu.VMEM((SUBLANES - 1, tc + HIST - SUBLANES, LANES), F32),
                        pltpu.VMEM((tc, CONV_WIDTH), F32)],
        compiler_params=_params(("arbitrary", "arbitrary")),
        name="conformer_conv",
    )(glu, glu, pw_b.reshape(1, -1), pw_b.reshape(1, -1), dw_w,
      dw_b.reshape(1, -1), ln_w.reshape(1, -1), ln_b.reshape(1, -1))


def _pad_lanes(v, offset):
    out = jnp.zeros((1, LANES), F32)
    return lax.dynamic_update_slice(out, v.reshape(1, -1).astype(F32), (0, offset))


def kernel(x, pre_mix_norm, w_in, gdn_conv_w, gdn_a_log, gdn_dt_bias, gdn_norm_w, cm_pw_b, cm_dw_w,
           cm_dw_b, cm_ln_w, cm_ln_b, w_out, post_mix_norm, pre_ffn_norm, w_gate, w_up, w_down,
           post_ffn_norm):
    bsz, t_len, d = x.shape
    depth = w_in.shape[0]
    m = bsz * t_len
    xf = x.reshape(m, d)
    o_z = QKV_WIDTH + GDN_WIDTH
    o_g = o_z + 2 * GDN_HEADS
    tn_in = 2 * MXU_WIDTH
    n_gdn = o_z + tn_in

    w_in_t = jnp.swapaxes(w_in, 1, 2)
    w_glu_t = w_in_t[:, o_g:, :].astype(BF16)

    h = rmsnorm_bf16(xf, pre_mix_norm[0])
    for l in range(depth):
        proj = matmul_nt(h, w_in_t, l, n_gdn, 2048, tn_in, name="in_proj_gdn", row_buffers=1)
        glu = matmul_nt(h, w_glu_t, l, 2 * CONV_WIDTH, 2048, tn_in, name="in_proj_glu")

        o_a = gdn_heads(proj, gdn_conv_w[l], _pad_lanes(gdn_a_log[l], GDN_HEADS),
                        _pad_lanes(gdn_dt_bias[l], GDN_HEADS), gdn_norm_w[l].reshape(1, HEAD_DIM),
                        bsz, t_len)
        c = conformer_conv(glu, cm_pw_b[l], cm_dw_w[l], cm_dw_b[l], cm_ln_w[l], cm_ln_b[l], bsz, t_len)

        mix = matmul_cat(o_a, c, w_out, l, 2048, tn_in, BF16, name="out_proj")
        xf, hf = residual_norm(mix, xf, post_mix_norm[l], pre_ffn_norm[l])

        act = ffn_up(hf, w_gate, w_up, l, 2048, MXU_WIDTH)
        ff = matmul(act, w_down, l, 1024, MXU_WIDTH, BF16, name="ffn_down")
        if l + 1 < depth:
            xf, h = residual_norm(ff, xf, post_ffn_norm[l], pre_mix_norm[l + 1])
        else:
            xf = residual_norm(ff, xf, post_ffn_norm[l])
    return xf.reshape(bsz, t_len, d)
```

```python
import functools

import jax
import jax.numpy as jnp
from jax import lax
from jax.experimental import pallas as pl
from jax.experimental.pallas import tpu as pltpu

D_MODEL = 4096
CHUNK = 64
HEAD_DIM = 128
GDN_WIDTH = D_MODEL // 2
GDN_HEADS = GDN_WIDTH // HEAD_DIM
QKV_WIDTH = 3 * GDN_WIDTH
SHORT_CONV = 4
CONV_WIDTH = D_MODEL - GDN_WIDTH
DW_KERNEL = 31
RMS_EPS = 1e-6
LN_EPS = 1e-5
L2_EPS = 1e-6

LANES = 128
SUBLANES = 8
MXU_WIDTH = 256
VMEM_LIMIT = 56 * 1024 * 1024

F32 = jnp.float32
BF16 = jnp.bfloat16


def _params(sem):
    return pltpu.CompilerParams(dimension_semantics=sem, vmem_limit_bytes=VMEM_LIMIT)


def _sigmoid(x):
    return 0.5 * jnp.tanh(0.5 * x) + 0.5


def _silu(x):
    hx = 0.5 * x
    return hx * jnp.tanh(hx) + hx


def _rms_kernel(x_ref, w_ref, h_ref):
    x = x_ref[...]
    r = lax.rsqrt(jnp.mean(x * x, axis=-1, keepdims=True) + RMS_EPS)
    h_ref[...] = (x * r * w_ref[...]).astype(h_ref.dtype)


def rmsnorm_bf16(x, w, tr=512):
    m, d = x.shape
    return pl.pallas_call(
        _rms_kernel,
        grid=(m // tr,),
        in_specs=[pl.BlockSpec((tr, d), lambda i: (i, 0)),
                  pl.BlockSpec((1, d), lambda i: (0, 0))],
        out_specs=pl.BlockSpec((tr, d), lambda i: (i, 0)),
        out_shape=jax.ShapeDtypeStruct((m, d), BF16),
        compiler_params=_params(("arbitrary",)),
        name="rmsnorm",
    )(x, w.reshape(1, d))


def _row_block(tm, k, buffers=2):
    return pl.BlockSpec((tm, k), lambda i, j: (i, 0), pipeline_mode=pl.Buffered(buffers))


def _dot_nt(a, b):
    return lax.dot_general(a, b, (((1,), (1,)), ((), ())), preferred_element_type=F32)


def _dot_tn(a, b):
    return lax.dot_general(a, b, (((0,), (0,)), ((), ())), preferred_element_type=F32)


def _mm_kernel(a_ref, w_ref, o_ref):
    o_ref[...] = jnp.dot(a_ref[...], w_ref[...].astype(BF16),
                         preferred_element_type=F32).astype(o_ref.dtype)


def _mm_nt_kernel(a_ref, wt_ref, o_ref):
    o_ref[...] = _dot_nt(a_ref[...], wt_ref[0].astype(BF16))


def matmul(a, w_stack, layer, tm, tn, out_dtype, name):
    m, k = a.shape
    n = w_stack.shape[-1]
    return pl.pallas_call(
        _mm_kernel,
        grid=(m // tm, n // tn),
        in_specs=[_row_block(tm, k, buffers=1),
                  pl.BlockSpec((None, k, tn), lambda i, j: (layer, 0, j))],
        out_specs=pl.BlockSpec((tm, tn), lambda i, j: (i, j)),
        out_shape=jax.ShapeDtypeStruct((m, n), out_dtype),
        compiler_params=_params(("arbitrary", "arbitrary")),
        name=name,
    )(a, w_stack)


def matmul_nt(a, wt_stack, layer, n, tm, tn, name, row_buffers=2, row0=0):
    m, k = a.shape
    return pl.pallas_call(
        _mm_nt_kernel,
        grid=(m // tm, n // tn),
        in_specs=[_row_block(tm, k, row_buffers),
                  pl.BlockSpec((pl.Element(1), pl.Element(tn), pl.Element(k)),
                               lambda i, j: (layer, pl.multiple_of(row0 + j * tn, SUBLANES), 0))],
        out_specs=pl.BlockSpec((tm, tn), lambda i, j: (i, j)),
        out_shape=jax.ShapeDtypeStruct((m, n), F32),
        compiler_params=_params(("arbitrary", "arbitrary")),
        name=name,
    )(a, wt_stack)


def _mm_cat_kernel(a1_ref, a2_ref, w1_ref, w2_ref, o_ref):
    acc = jnp.dot(a1_ref[...], w1_ref[...].astype(BF16), preferred_element_type=F32)
    acc += jnp.dot(a2_ref[...], w2_ref[...].astype(BF16), preferred_element_type=F32)
    o_ref[...] = acc.astype(o_ref.dtype)


def matmul_cat(a1, a2, w_stack, layer, tm, tn, out_dtype, name):
    m, k1 = a1.shape
    _, k2 = a2.shape
    n = w_stack.shape[-1]
    assert k1 == k2
    return pl.pallas_call(
        _mm_cat_kernel,
        grid=(m // tm, n // tn),
        in_specs=[_row_block(tm, k1, buffers=1), _row_block(tm, k2, buffers=1),
                  pl.BlockSpec((None, k1, tn), lambda i, j: (layer, 0, j)),
                  pl.BlockSpec((None, k2, tn), lambda i, j: (layer, 1, j))],
        out_specs=pl.BlockSpec((tm, tn), lambda i, j: (i, j)),
        out_shape=jax.ShapeDtypeStruct((m, n), out_dtype),
        compiler_params=_params(("arbitrary", "arbitrary")),
        name=name,
    )(a1, a2, w_stack, w_stack)


def _ffn_up_kernel(h_ref, wg_ref, wu_ref, o_ref):
    h = h_ref[...]
    g = jnp.dot(h, wg_ref[...].astype(BF16), preferred_element_type=F32)
    u = jnp.dot(h, wu_ref[...].astype(BF16), preferred_element_type=F32)
    o_ref[...] = (_silu(g) * u).astype(o_ref.dtype)


def ffn_up(h, wg_stack, wu_stack, layer, tm, tn):
    m, k = h.shape
    n = wg_stack.shape[-1]
    wspec = pl.BlockSpec((None, k, tn), lambda i, j: (layer, 0, j))
    return pl.pallas_call(
        _ffn_up_kernel,
        grid=(m // tm, n // tn),
        in_specs=[_row_block(tm, k, buffers=1), wspec, wspec],
        out_specs=pl.BlockSpec((tm, tn), lambda i, j: (i, j)),
        out_shape=jax.ShapeDtypeStruct((m, n), BF16),
        compiler_params=_params(("arbitrary", "arbitrary")),
        name="ffn_up",
    )(h, wg_stack, wu_stack)


def _post_kernel(y_ref, x_ref, wp_ref, wn_ref, xo_ref, h_ref):
    y = y_ref[...].astype(F32)
    r = lax.rsqrt(jnp.mean(y * y, axis=-1, keepdims=True) + RMS_EPS)
    xn = x_ref[...] + y * r * wp_ref[...]
    xo_ref[...] = xn
    r2 = lax.rsqrt(jnp.mean(xn * xn, axis=-1, keepdims=True) + RMS_EPS)
    h_ref[...] = (xn * r2 * wn_ref[...]).astype(h_ref.dtype)


def _post_last_kernel(y_ref, x_ref, wp_ref, xo_ref):
    y = y_ref[...].astype(F32)
    r = lax.rsqrt(jnp.mean(y * y, axis=-1, keepdims=True) + RMS_EPS)
    xo_ref[...] = x_ref[...] + y * r * wp_ref[...]


def residual_norm(y, x, w_post, w_next=None, tr=256):
    m, d = x.shape
    row = pl.BlockSpec((tr, d), lambda i: (i, 0))
    vec = pl.BlockSpec((1, d), lambda i: (0, 0))
    if w_next is None:
        return pl.pallas_call(
            _post_last_kernel, grid=(m // tr,),
            in_specs=[row, row, vec], out_specs=row,
            out_shape=jax.ShapeDtypeStruct((m, d), F32),
            compiler_params=_params(("arbitrary",)), name="residual_norm_last",
        )(y, x, w_post.reshape(1, d))
    return pl.pallas_call(
        _post_kernel, grid=(m // tr,),
        in_specs=[row, row, vec, vec], out_specs=[row, row],
        out_shape=[jax.ShapeDtypeStruct((m, d), F32), jax.ShapeDtypeStruct((m, d), BF16)],
        compiler_params=_params(("arbitrary",)), name="residual_norm",
    )(y, x, w_post.reshape(1, d), w_next.reshape(1, d))


def _dot_f32(a, b):
    return jnp.dot(a, b, precision=lax.Precision.HIGHEST, preferred_element_type=F32)


def _gdn_kernel(q_ref, k_ref, v_ref, z_ref, ba_ref, cwq_ref, cwk_ref, cwv_ref,
                alog_ref, dtb_ref, nw_ref, o_ref,
                s_ref, qpad, kpad, vpad, qa, ka, va):
    t = pl.program_id(1)
    pad0 = SUBLANES

    @pl.when(t == 0)
    def _():
        s_ref[...] = jnp.zeros_like(s_ref)
        qpad[0:pad0, :] = jnp.zeros((pad0, GDN_WIDTH), F32)
        kpad[0:pad0, :] = jnp.zeros((pad0, GDN_WIDTH), F32)
        vpad[0:pad0, :] = jnp.zeros((pad0, GDN_WIDTH), F32)

    def conv_silu(pad, x_ref, cw_ref, dst):
        pad[pad0:pad0 + CHUNK, :] = x_ref[...]
        acc = None
        for j in range(SHORT_CONV):
            off = pad0 - (SHORT_CONV - 1) + j
            term = pad[off:off + CHUNK, :] * cw_ref[j:j + 1, :]
            acc = term if acc is None else acc + term
        dst[...] = _silu(acc)
        pad[0:pad0, :] = pad[CHUNK:CHUNK + pad0, :]

    conv_silu(qpad, q_ref, cwq_ref, qa)
    conv_silu(kpad, k_ref, cwk_ref, ka)
    conv_silu(vpad, v_ref, cwv_ref, va)

    ii = lax.broadcasted_iota(jnp.int32, (CHUNK, CHUNK), 0)
    jj = lax.broadcasted_iota(jnp.int32, (CHUNK, CHUNK), 1)
    causal = jj <= ii
    strict = jj < ii
    eye = jnp.where(ii == jj, 1.0, 0.0).astype(F32)
    level_masks = []
    s_blk = 1
    while s_blk < CHUNK:
        level_masks.append(((ii // (2 * s_blk)) == (jj // (2 * s_blk))) & ((ii // s_blk) != (jj // s_blk)))
        s_blk *= 2
    nw = nw_ref[...]

    def bdot(a, b):
        return jnp.dot(a.astype(BF16), b.astype(BF16), preferred_element_type=F32)

    ba = ba_ref[...]
    beta_all = _sigmoid(ba)
    x = ba + dtb_ref[...]
    softplus = jnp.maximum(x, 0.0) + jnp.log1p(jnp.exp(-jnp.abs(x)))
    g_all = -jnp.exp(alog_ref[...]) * softplus
    gcum = _dot_f32(jnp.where(causal, 1.0, 0.0).astype(F32), g_all)
    gcum_t = gcum.T

    def prep(i):
        cs = slice(i * HEAD_DIM, (i + 1) * HEAD_DIM)
        gcol = gcum[:, GDN_HEADS + i:GDN_HEADS + i + 1]
        grow = gcum_t[GDN_HEADS + i:GDN_HEADS + i + 1, :]
        bcol = beta_all[:, i:i + 1]
        glast = gcol[CHUNK - 1:CHUNK, :]
        q = qa[:, cs]
        k = ka[:, cs]
        q = q * (lax.rsqrt(jnp.sum(q * q, axis=-1, keepdims=True) + L2_EPS) * (HEAD_DIM ** -0.5))
        k = k * lax.rsqrt(jnp.sum(k * k, axis=-1, keepdims=True) + L2_EPS)
        eg = jnp.exp(gcol)
        kbeta = k * bcol
        return dict(
            decay=jnp.exp(jnp.where(causal, gcol - grow, -jnp.inf)),
            k16=k.astype(BF16), qkb=jnp.concatenate([q, kbeta], axis=0).astype(BF16),
            rhs=jnp.concatenate([va[:, cs] * bcol, kbeta * eg], axis=1).astype(BF16),
            qg=q * eg, kd=(k * jnp.exp(glast - gcol)).astype(BF16), dlast=jnp.exp(glast))

    pr = [prep(i) for i in range(GDN_HEADS)]
    qk_kk = [_dot_nt(p["qkb"], p["k16"]) for p in pr]
    a_intra = [(x[:CHUNK] * p["decay"]).astype(BF16) for x, p in zip(qk_kk, pr)]
    lower = [jnp.where(strict, x[CHUNK:] * p["decay"], 0.0) for x, p in zip(qk_kk, pr)]

    t_inv = [eye - jnp.where(level_masks[0], lo, 0.0) for lo in lower]
    for mask in level_masks[1:]:
        y = [bdot(jnp.where(mask, lo, 0.0), t) for lo, t in zip(lower, t_inv)]
        t_inv = [t - bdot(t, yy) for t, yy in zip(t_inv, y)]
    uw = [jnp.dot(t.astype(BF16), p["rhs"], preferred_element_type=F32) for t, p in zip(t_inv, pr)]
    wq = [jnp.concatenate([x[:, HEAD_DIM:], p["qg"]], axis=0).astype(BF16) for x, p in zip(uw, pr)]

    ws_qs = [jnp.dot(wq[i], s_ref[i].astype(BF16), preferred_element_type=F32)
             for i in range(GDN_HEADS)]
    vn16 = [(uw[i][:, :HEAD_DIM] - ws_qs[i][:CHUNK]).astype(BF16) for i in range(GDN_HEADS)]
    av = [jnp.dot(a_intra[i], vn16[i], preferred_element_type=F32) for i in range(GDN_HEADS)]
    kv = [_dot_tn(pr[i]["kd"], vn16[i]) for i in range(GDN_HEADS)]
    for i in range(GDN_HEADS):
        cs = slice(i * HEAD_DIM, (i + 1) * HEAD_DIM)
        s_ref[i] = s_ref[i] * pr[i]["dlast"] + kv[i]
        o = ws_qs[i][CHUNK:] + av[i]
        o = o * lax.rsqrt(jnp.mean(o * o, axis=-1, keepdims=True) + RMS_EPS) * nw
        o_ref[:, cs] = (o * _silu(z_ref[:, cs])).astype(o_ref.dtype)


def gdn_heads(proj, conv_w, a_log_row, dt_bias_row, norm_w, bsz, t_len):
    m = bsz * t_len
    nt = t_len // CHUNK
    col = lambda off: pl.BlockSpec((CHUNK, GDN_WIDTH), lambda b, t: (b * nt + t, off))
    cw = lambda off: pl.BlockSpec((SHORT_CONV, GDN_WIDTH), lambda b, t: (0, off))
    vec = pl.BlockSpec((1, LANES), lambda b, t: (0, 0))
    gate_block = (QKV_WIDTH + GDN_WIDTH) // LANES
    return pl.pallas_call(
        _gdn_kernel,
        grid=(bsz, nt),
        in_specs=[col(0), col(1), col(2), col(3),
                  pl.BlockSpec((CHUNK, LANES), lambda b, t: (b * nt + t, gate_block)),
                  cw(0), cw(1), cw(2), vec, vec, vec],
        out_specs=pl.BlockSpec((CHUNK, GDN_WIDTH), lambda b, t: (b * nt + t, 0)),
        out_shape=jax.ShapeDtypeStruct((m, GDN_WIDTH), BF16),
        scratch_shapes=[
            pltpu.VMEM((GDN_HEADS, HEAD_DIM, HEAD_DIM), F32),
            pltpu.VMEM((CHUNK + SUBLANES, GDN_WIDTH), F32),
            pltpu.VMEM((CHUNK + SUBLANES, GDN_WIDTH), F32),
            pltpu.VMEM((CHUNK + SUBLANES, GDN_WIDTH), F32),
            pltpu.VMEM((CHUNK, GDN_WIDTH), F32),
            pltpu.VMEM((CHUNK, GDN_WIDTH), F32),
            pltpu.VMEM((CHUNK, GDN_WIDTH), F32),
        ],
        compiler_params=_params(("arbitrary", "arbitrary")),
        name="gdn_heads",
    )(proj, proj, proj, proj, proj, conv_w, conv_w, conv_w, a_log_row, dt_bias_row, norm_w)


HIST = 32


def _cconv_kernel(val_ref, gate_ref, bv_ref, bg_ref, dww_ref, dwb_ref, lnw_ref, lnb_ref, o_ref,
                  pad, sh, conv, *, tc, ct, rt):
    t = pl.program_id(1)
    lead = HIST - (DW_KERNEL - 1)
    nsh = tc + HIST - SUBLANES
    ncs = CONV_WIDTH // LANES

    @pl.when(t == 0)
    def _():
        pad[0:HIST, :] = jnp.zeros((HIST, CONV_WIDTH), F32)

    pad[HIST:HIST + tc, :] = (val_ref[...] + bv_ref[...]) * _sigmoid(gate_ref[...] + bg_ref[...])

    def stripe(cs, carry):
        c0 = pl.multiple_of(cs * LANES, LANES)
        for r in range(1, SUBLANES):
            sh[r - 1] = pad[r:r + nsh, pl.ds(c0, LANES)]
        for rr in range(tc // ct):
            acc = jnp.broadcast_to(dwb_ref[:, pl.ds(c0, LANES)], (ct, LANES))
            for j in range(DW_KERNEL):
                r = (lead + j) % SUBLANES
                row = rr * ct + lead + j - r
                if r == 0:
                    xs = pad[row:row + ct, pl.ds(c0, LANES)]
                else:
                    xs = sh[r - 1, row:row + ct, :]
                acc = acc + xs * dww_ref[j:j + 1, pl.ds(c0, LANES)]
            conv[rr * ct:(rr + 1) * ct, pl.ds(c0, LANES)] = acc
        return carry

    lax.fori_loop(0, ncs, stripe, 0)
    pad[0:HIST, :] = pad[tc:tc + HIST, :]

    ln_unroll = 4

    def rows(r, carry):
        r0s = [pl.multiple_of((r * ln_unroll + u) * rt, rt) for u in range(ln_unroll)]
        ys = [conv[pl.ds(r0, rt), :] for r0 in r0s]
        mus = [jnp.mean(y, axis=-1, keepdims=True) for y in ys]
        ycs = [y - mu for y, mu in zip(ys, mus)]
        vrs = [jnp.mean(yc * yc, axis=-1, keepdims=True) for yc in ycs]
        for r0, yc, var in zip(r0s, ycs, vrs):
            yn = yc * lax.rsqrt(var + LN_EPS) * lnw_ref[...] + lnb_ref[...]
            o_ref[pl.ds(r0, rt), :] = _silu(yn).astype(o_ref.dtype)
        return carry

    lax.fori_loop(0, tc // (rt * ln_unroll), rows, 0)


def conformer_conv(glu, pw_b, dw_w, dw_b, ln_w, ln_b, bsz, t_len, tc=256, ct=64, rt=32):
    m = bsz * t_len
    nt = t_len // tc
    kern = functools.partial(_cconv_kernel, tc=tc, ct=ct, rt=rt)
    half = lambda c: pl.BlockSpec((tc, CONV_WIDTH), lambda b, t: (b * nt + t, c))
    bias = lambda c: pl.BlockSpec((1, CONV_WIDTH), lambda b, t: (0, c))
    vec = pl.BlockSpec((1, CONV_WIDTH), lambda b, t: (0, 0))
    return pl.pallas_call(
        kern,
        grid=(bsz, nt),
        in_specs=[half(0), half(1), bias(0), bias(1),
                  pl.BlockSpec((DW_KERNEL, CONV_WIDTH), lambda b, t: (0, 0)), vec, vec, vec],
        out_specs=pl.BlockSpec((tc, CONV_WIDTH), lambda b, t: (b * nt + t, 0)),
        out_shape=jax.ShapeDtypeStruct((m, CONV_WIDTH), BF16),
        scratch_shapes=[pltpu.VMEM((tc + HIST, CONV_WIDTH), F32),
                        pltpu.VMEM((SUBLANES - 1, tc + HIST - SUBLANES, LANES), F32),
                        pltpu.VMEM((tc, CONV_WIDTH), F32)],
        compiler_params=_params(("arbitrary", "arbitrary")),
        name="conformer_conv",
    )(glu, glu, pw_b.reshape(1, -1), pw_b.reshape(1, -1), dw_w,
      dw_b.reshape(1, -1), ln_w.reshape(1, -1), ln_b.reshape(1, -1))


def _pad_lanes(v, offset):
    out = jnp.zeros((1, LANES), F32)
    return lax.dynamic_update_slice(out, v.reshape(1, -1).astype(F32), (0, offset))


def kernel(x, pre_mix_norm, w_in, gdn_conv_w, gdn_a_log, gdn_dt_bias, gdn_norm_w, cm_pw_b, cm_dw_w,
           cm_dw_b, cm_ln_w, cm_ln_b, w_out, post_mix_norm, pre_ffn_norm, w_gate, w_up, w_down,
           post_ffn_norm):
    bsz, t_len, d = x.shape
    depth = w_in.shape[0]
    m = bsz * t_len
    xf = x.reshape(m, d)
    o_z = QKV_WIDTH + GDN_WIDTH
    o_g = o_z + 2 * GDN_HEADS
    tn_in = 2 * MXU_WIDTH
    n_gdn = o_z + tn_in

    w_in_t = jnp.swapaxes(w_in, 1, 2)

    h = rmsnorm_bf16(xf, pre_mix_norm[0])
    for l in range(depth):
        proj = matmul_nt(h, w_in_t, l, n_gdn, 2048, tn_in, name="in_proj_gdn", row_buffers=1)
        glu = matmul_nt(h, w_in_t, l, 2 * CONV_WIDTH, 2048, MXU_WIDTH, name="in_proj_glu", row0=o_g)

        o_a = gdn_heads(proj, gdn_conv_w[l], _pad_lanes(gdn_a_log[l], GDN_HEADS),
                        _pad_lanes(gdn_dt_bias[l], GDN_HEADS), gdn_norm_w[l].reshape(1, HEAD_DIM),
                        bsz, t_len)
        c = conformer_conv(glu, cm_pw_b[l], cm_dw_w[l], cm_dw_b[l], cm_ln_w[l], cm_ln_b[l], bsz, t_len)

        mix = matmul_cat(o_a, c, w_out, l, 2048, tn_in, BF16, name="out_proj")
        xf, hf = residual_norm(mix, xf, post_mix_norm[l], pre_ffn_norm[l])

        act = ffn_up(hf, w_gate, w_up, l, 2048, MXU_WIDTH)
        ff = matmul(act, w_down, l, 1024, MXU_WIDTH, BF16, name="ffn_down")
        if l + 1 < depth:
            xf, h = residual_norm(ff, xf, post_ffn_norm[l], pre_mix_norm[l + 1])
        else:
            xf = residual_norm(ff, xf, post_ffn_norm[l])
    return xf.reshape(bsz, t_len, d)
```

```python
import functools

import jax
import jax.numpy as jnp
from jax import lax
from jax.experimental import pallas as pl
from jax.experimental.pallas import tpu as pltpu

D_MODEL = 4096
CHUNK = 64
HEAD_DIM = 128
GDN_WIDTH = D_MODEL // 2
GDN_HEADS = GDN_WIDTH // HEAD_DIM
QKV_WIDTH = 3 * GDN_WIDTH
SHORT_CONV = 4
CONV_WIDTH = D_MODEL - GDN_WIDTH
DW_KERNEL = 31
RMS_EPS = 1e-6
LN_EPS = 1e-5
L2_EPS = 1e-6

LANES = 128
SUBLANES = 8
MXU_WIDTH = 256
VMEM_LIMIT = 56 * 1024 * 1024

F32 = jnp.float32
BF16 = jnp.bfloat16


def _params(sem):
    return pltpu.CompilerParams(dimension_semantics=sem, vmem_limit_bytes=VMEM_LIMIT)


def _sigmoid(x):
    return 0.5 * jnp.tanh(0.5 * x) + 0.5


def _silu(x):
    hx = 0.5 * x
    return hx * jnp.tanh(hx) + hx


def _rms_kernel(x_ref, w_ref, h_ref):
    x = x_ref[...]
    r = lax.rsqrt(jnp.mean(x * x, axis=-1, keepdims=True) + RMS_EPS)
    h_ref[...] = (x * r * w_ref[...]).astype(h_ref.dtype)


def rmsnorm_bf16(x, w, tr=512):
    m, d = x.shape
    return pl.pallas_call(
        _rms_kernel,
        grid=(m // tr,),
        in_specs=[pl.BlockSpec((tr, d), lambda i: (i, 0)),
                  pl.BlockSpec((1, d), lambda i: (0, 0))],
        out_specs=pl.BlockSpec((tr, d), lambda i: (i, 0)),
        out_shape=jax.ShapeDtypeStruct((m, d), BF16),
        compiler_params=_params(("arbitrary",)),
        name="rmsnorm",
    )(x, w.reshape(1, d))


def _row_block(tm, k, buffers=2):
    return pl.BlockSpec((tm, k), lambda i, j: (i, 0), pipeline_mode=pl.Buffered(buffers))


def _dot_nt(a, b):
    return lax.dot_general(a, b, (((1,), (1,)), ((), ())), preferred_element_type=F32)


def _dot_tn(a, b):
    return lax.dot_general(a, b, (((0,), (0,)), ((), ())), preferred_element_type=F32)


def _mm_kernel(a_ref, w_ref, o_ref):
    o_ref[...] = jnp.dot(a_ref[...], w_ref[...].astype(BF16),
                         preferred_element_type=F32).astype(o_ref.dtype)


def _mm_nt_kernel(a_ref, wt_ref, o_ref):
    o_ref[...] = _dot_nt(a_ref[...], wt_ref[...].astype(BF16))


def matmul(a, w_stack, layer, tm, tn, out_dtype, name):
    m, k = a.shape
    n = w_stack.shape[-1]
    return pl.pallas_call(
        _mm_kernel,
        grid=(m // tm, n // tn),
        in_specs=[_row_block(tm, k, buffers=1),
                  pl.BlockSpec((None, k, tn), lambda i, j: (layer, 0, j))],
        out_specs=pl.BlockSpec((tm, tn), lambda i, j: (i, j)),
        out_shape=jax.ShapeDtypeStruct((m, n), out_dtype),
        compiler_params=_params(("arbitrary", "arbitrary")),
        name=name,
    )(a, w_stack)


def matmul_nt(a, wt_stack, layer, n, tm, tn, name, row_buffers=2):
    m, k = a.shape
    return pl.pallas_call(
        _mm_nt_kernel,
        grid=(m // tm, n // tn),
        in_specs=[_row_block(tm, k, row_buffers),
                  pl.BlockSpec((None, tn, k), lambda i, j: (layer, j, 0))],
        out_specs=pl.BlockSpec((tm, tn), lambda i, j: (i, j)),
        out_shape=jax.ShapeDtypeStruct((m, n), F32),
        compiler_params=_params(("arbitrary", "arbitrary")),
        name=name,
    )(a, wt_stack)


def _mm_cat_kernel(a1_ref, a2_ref, w1_ref, w2_ref, o_ref):
    acc = jnp.dot(a1_ref[...], w1_ref[...].astype(BF16), preferred_element_type=F32)
    acc += jnp.dot(a2_ref[...], w2_ref[...].astype(BF16), preferred_element_type=F32)
    o_ref[...] = acc.astype(o_ref.dtype)


def matmul_cat(a1, a2, w_stack, layer, tm, tn, out_dtype, name):
    m, k1 = a1.shape
    _, k2 = a2.shape
    n = w_stack.shape[-1]
    assert k1 == k2
    return pl.pallas_call(
        _mm_cat_kernel,
        grid=(m // tm, n // tn),
        in_specs=[_row_block(tm, k1, buffers=1), _row_block(tm, k2, buffers=1),
                  pl.BlockSpec((None, k1, tn), lambda i, j: (layer, 0, j)),
                  pl.BlockSpec((None, k2, tn), lambda i, j: (layer, 1, j))],
        out_specs=pl.BlockSpec((tm, tn), lambda i, j: (i, j)),
        out_shape=jax.ShapeDtypeStruct((m, n), out_dtype),
        compiler_params=_params(("arbitrary", "arbitrary")),
        name=name,
    )(a1, a2, w_stack, w_stack)


def _ffn_up_kernel(h_ref, wg_ref, wu_ref, o_ref):
    h = h_ref[...]
    g = jnp.dot(h, wg_ref[...].astype(BF16), preferred_element_type=F32)
    u = jnp.dot(h, wu_ref[...].astype(BF16), preferred_element_type=F32)
    o_ref[...] = (_silu(g) * u).astype(o_ref.dtype)


def ffn_up(h, wg_stack, wu_stack, layer, tm, tn):
    m, k = h.shape
    n = wg_stack.shape[-1]
    wspec = pl.BlockSpec((None, k, tn), lambda i, j: (layer, 0, j))
    return pl.pallas_call(
        _ffn_up_kernel,
        grid=(m // tm, n // tn),
        in_specs=[_row_block(tm, k, buffers=1), wspec, wspec],
        out_specs=pl.BlockSpec((tm, tn), lambda i, j: (i, j)),
        out_shape=jax.ShapeDtypeStruct((m, n), BF16),
        compiler_params=_params(("arbitrary", "arbitrary")),
        name="ffn_up",
    )(h, wg_stack, wu_stack)


def _post_kernel(y_ref, x_ref, wp_ref, wn_ref, xo_ref, h_ref):
    y = y_ref[...].astype(F32)
    r = lax.rsqrt(jnp.mean(y * y, axis=-1, keepdims=True) + RMS_EPS)
    xn = x_ref[...] + y * r * wp_ref[...]
    xo_ref[...] = xn
    r2 = lax.rsqrt(jnp.mean(xn * xn, axis=-1, keepdims=True) + RMS_EPS)
    h_ref[...] = (xn * r2 * wn_ref[...]).astype(h_ref.dtype)


def _post_last_kernel(y_ref, x_ref, wp_ref, xo_ref):
    y = y_ref[...].astype(F32)
    r = lax.rsqrt(jnp.mean(y * y, axis=-1, keepdims=True) + RMS_EPS)
    xo_ref[...] = x_ref[...] + y * r * wp_ref[...]


def residual_norm(y, x, w_post, w_next=None, tr=256):
    m, d = x.shape
    row = pl.BlockSpec((tr, d), lambda i: (i, 0))
    vec = pl.BlockSpec((1, d), lambda i: (0, 0))
    if w_next is None:
        return pl.pallas_call(
            _post_last_kernel, grid=(m // tr,),
            in_specs=[row, row, vec], out_specs=row,
            out_shape=jax.ShapeDtypeStruct((m, d), F32),
            compiler_params=_params(("arbitrary",)), name="residual_norm_last",
        )(y, x, w_post.reshape(1, d))
    return pl.pallas_call(
        _post_kernel, grid=(m // tr,),
        in_specs=[row, row, vec, vec], out_specs=[row, row],
        out_shape=[jax.ShapeDtypeStruct((m, d), F32), jax.ShapeDtypeStruct((m, d), BF16)],
        compiler_params=_params(("arbitrary",)), name="residual_norm",
    )(y, x, w_post.reshape(1, d), w_next.reshape(1, d))


STEP_CHUNKS = 4


def _dot_f32(a, b):
    return jnp.dot(a, b, precision=lax.Precision.HIGHEST, preferred_element_type=F32)


def _gdn_kernel(q_ref, k_ref, v_ref, z_ref, ba_ref, cwq_ref, cwk_ref, cwv_ref,
                alog_ref, dtb_ref, nw_ref, o_ref,
                s_ref, qpad, kpad, vpad, qa, ka, va):
    t = pl.program_id(1)
    pad0 = SUBLANES
    rows = STEP_CHUNKS * CHUNK

    @pl.when(t == 0)
    def _():
        s_ref[...] = jnp.zeros_like(s_ref)
        qpad[0:pad0, :] = jnp.zeros((pad0, GDN_WIDTH), F32)
        kpad[0:pad0, :] = jnp.zeros((pad0, GDN_WIDTH), F32)
        vpad[0:pad0, :] = jnp.zeros((pad0, GDN_WIDTH), F32)

    def conv_silu(pad, x_ref, cw_ref, dst):
        pad[pad0:pad0 + rows, :] = x_ref[...]
        acc = None
        for j in range(SHORT_CONV):
            off = pad0 - (SHORT_CONV - 1) + j
            term = pad[off:off + rows, :] * cw_ref[j:j + 1, :]
            acc = term if acc is None else acc + term
        dst[...] = _silu(acc)
        pad[0:pad0, :] = pad[rows:rows + pad0, :]

    conv_silu(qpad, q_ref, cwq_ref, qa)
    conv_silu(kpad, k_ref, cwk_ref, ka)
    conv_silu(vpad, v_ref, cwv_ref, va)

    ii = lax.broadcasted_iota(jnp.int32, (CHUNK, CHUNK), 0)
    jj = lax.broadcasted_iota(jnp.int32, (CHUNK, CHUNK), 1)
    causal = jj <= ii
    strict = jj < ii
    eye = jnp.where(ii == jj, 1.0, 0.0).astype(F32)
    level_masks = []
    s_blk = 1
    while s_blk < CHUNK:
        level_masks.append(((ii // (2 * s_blk)) == (jj // (2 * s_blk))) & ((ii // s_blk) != (jj // s_blk)))
        s_blk *= 2
    nw = nw_ref[...]

    def bdot(a, b):
        return jnp.dot(a.astype(BF16), b.astype(BF16), preferred_element_type=F32)

    tri = jnp.where(causal, 1.0, 0.0).astype(F32)

    def chunk_update(r0):
        rsl = slice(r0, r0 + CHUNK)
        ba = ba_ref[rsl, :]
        beta_all = _sigmoid(ba)
        x = ba + dtb_ref[...]
        softplus = jnp.maximum(x, 0.0) + jnp.log1p(jnp.exp(-jnp.abs(x)))
        g_all = -jnp.exp(alog_ref[...]) * softplus
        gcum = _dot_f32(tri, g_all)
        gcum_t = gcum.T

        def prep(i):
            cs = slice(i * HEAD_DIM, (i + 1) * HEAD_DIM)
            gcol = gcum[:, GDN_HEADS + i:GDN_HEADS + i + 1]
            grow = gcum_t[GDN_HEADS + i:GDN_HEADS + i + 1, :]
            bcol = beta_all[:, i:i + 1]
            glast = gcol[CHUNK - 1:CHUNK, :]
            q = qa[rsl, cs]
            k = ka[rsl, cs]
            q = q * (lax.rsqrt(jnp.sum(q * q, axis=-1, keepdims=True) + L2_EPS) * (HEAD_DIM ** -0.5))
            k = k * lax.rsqrt(jnp.sum(k * k, axis=-1, keepdims=True) + L2_EPS)
            eg = jnp.exp(gcol)
            kbeta = k * bcol
            return dict(
                decay=jnp.exp(jnp.where(causal, gcol - grow, -jnp.inf)),
                k16=k.astype(BF16), qkb=jnp.concatenate([q, kbeta], axis=0).astype(BF16),
                rhs=jnp.concatenate([va[rsl, cs] * bcol, kbeta * eg], axis=1).astype(BF16),
                qg=q * eg, kd=(k * jnp.exp(glast - gcol)).astype(BF16), dlast=jnp.exp(glast))

        pr = [prep(i) for i in range(GDN_HEADS)]
        qk_kk = [_dot_nt(p["qkb"], p["k16"]) for p in pr]
        a_intra = [(x[:CHUNK] * p["decay"]).astype(BF16) for x, p in zip(qk_kk, pr)]
        lower = [jnp.where(strict, x[CHUNK:] * p["decay"], 0.0) for x, p in zip(qk_kk, pr)]

        t_inv = [eye - jnp.where(level_masks[0], lo, 0.0) for lo in lower]
        for mask in level_masks[1:]:
            y = [bdot(jnp.where(mask, lo, 0.0), t) for lo, t in zip(lower, t_inv)]
            t_inv = [t - bdot(t, yy) for t, yy in zip(t_inv, y)]
        uw = [jnp.dot(t.astype(BF16), p["rhs"], preferred_element_type=F32) for t, p in zip(t_inv, pr)]
        wq = [jnp.concatenate([x[:, HEAD_DIM:], p["qg"]], axis=0).astype(BF16) for x, p in zip(uw, pr)]

        ws_qs = [jnp.dot(wq[i], s_ref[i].astype(BF16), preferred_element_type=F32)
                 for i in range(GDN_HEADS)]
        vn16 = [(uw[i][:, :HEAD_DIM] - ws_qs[i][:CHUNK]).astype(BF16) for i in range(GDN_HEADS)]
        av = [jnp.dot(a_intra[i], vn16[i], preferred_element_type=F32) for i in range(GDN_HEADS)]
        kv = [_dot_tn(pr[i]["kd"], vn16[i]) for i in range(GDN_HEADS)]
        for i in range(GDN_HEADS):
            cs = slice(i * HEAD_DIM, (i + 1) * HEAD_DIM)
            s_ref[i] = s_ref[i] * pr[i]["dlast"] + kv[i]
            o = ws_qs[i][CHUNK:] + av[i]
            o = o * lax.rsqrt(jnp.mean(o * o, axis=-1, keepdims=True) + RMS_EPS) * nw
            o_ref[rsl, cs] = (o * _silu(z_ref[rsl, cs])).astype(o_ref.dtype)

    for c in range(STEP_CHUNKS):
        chunk_update(c * CHUNK)


def gdn_heads(proj, conv_w, a_log_row, dt_bias_row, norm_w, bsz, t_len):
    m = bsz * t_len
    rows = STEP_CHUNKS * CHUNK
    nt = t_len // rows
    col = lambda off: pl.BlockSpec((rows, GDN_WIDTH), lambda b, t: (b * nt + t, off))
    cw = lambda off: pl.BlockSpec((SHORT_CONV, GDN_WIDTH), lambda b, t: (0, off))
    vec = pl.BlockSpec((1, LANES), lambda b, t: (0, 0))
    gate_block = (QKV_WIDTH + GDN_WIDTH) // LANES
    return pl.pallas_call(
        _gdn_kernel,
        grid=(bsz, nt),
        in_specs=[col(0), col(1), col(2), col(3),
                  pl.BlockSpec((rows, LANES), lambda b, t: (b * nt + t, gate_block)),
                  cw(0), cw(1), cw(2), vec, vec, vec],
        out_specs=pl.BlockSpec((rows, GDN_WIDTH), lambda b, t: (b * nt + t, 0)),
        out_shape=jax.ShapeDtypeStruct((m, GDN_WIDTH), BF16),
        scratch_shapes=[
            pltpu.VMEM((GDN_HEADS, HEAD_DIM, HEAD_DIM), F32),
            pltpu.VMEM((rows + SUBLANES, GDN_WIDTH), F32),
            pltpu.VMEM((rows + SUBLANES, GDN_WIDTH), F32),
            pltpu.VMEM((rows + SUBLANES, GDN_WIDTH), F32),
            pltpu.VMEM((rows, GDN_WIDTH), F32),
            pltpu.VMEM((rows, GDN_WIDTH), F32),
            pltpu.VMEM((rows, GDN_WIDTH), F32),
        ],
        compiler_params=_params(("arbitrary", "arbitrary")),
        name="gdn_heads",
    )(proj, proj, proj, proj, proj, conv_w, conv_w, conv_w, a_log_row, dt_bias_row, norm_w)


HIST = 32


def _cconv_kernel(val_ref, gate_ref, bv_ref, bg_ref, dww_ref, dwb_ref, lnw_ref, lnb_ref, o_ref,
                  pad, sh, conv, *, tc, rt):
    t = pl.program_id(1)
    lead = HIST - (DW_KERNEL - 1)
    nsh = tc + HIST - SUBLANES
    ncs = CONV_WIDTH // LANES

    @pl.when(t == 0)
    def _():
        pad[0:HIST, :] = jnp.zeros((HIST, CONV_WIDTH), F32)

    pad[HIST:HIST + tc, :] = (val_ref[...] + bv_ref[...]) * _sigmoid(gate_ref[...] + bg_ref[...])

    def stripe(cs, carry):
        c0 = pl.multiple_of(cs * LANES, LANES)
        for r in range(1, SUBLANES):
            sh[r - 1] = pad[r:r + nsh, pl.ds(c0, LANES)]
        for rr in range(tc // rt):
            acc = jnp.broadcast_to(dwb_ref[:, pl.ds(c0, LANES)], (rt, LANES))
            for j in range(DW_KERNEL):
                r = (lead + j) % SUBLANES
                row = rr * rt + lead + j - r
                if r == 0:
                    xs = pad[row:row + rt, pl.ds(c0, LANES)]
                else:
                    xs = sh[r - 1, row:row + rt, :]
                acc = acc + xs * dww_ref[j:j + 1, pl.ds(c0, LANES)]
            conv[rr * rt:(rr + 1) * rt, pl.ds(c0, LANES)] = acc
        return carry

    lax.fori_loop(0, ncs, stripe, 0)
    pad[0:HIST, :] = pad[tc:tc + HIST, :]

    ln_unroll = 4

    def rows(r, carry):
        r0s = [pl.multiple_of((r * ln_unroll + u) * rt, rt) for u in range(ln_unroll)]
        ys = [conv[pl.ds(r0, rt), :] for r0 in r0s]
        mus = [jnp.mean(y, axis=-1, keepdims=True) for y in ys]
        ycs = [y - mu for y, mu in zip(ys, mus)]
        vrs = [jnp.mean(yc * yc, axis=-1, keepdims=True) for yc in ycs]
        for r0, yc, var in zip(r0s, ycs, vrs):
            yn = yc * lax.rsqrt(var + LN_EPS) * lnw_ref[...] + lnb_ref[...]
            o_ref[pl.ds(r0, rt), :] = _silu(yn).astype(o_ref.dtype)
        return carry

    lax.fori_loop(0, tc // (rt * ln_unroll), rows, 0)


def conformer_conv(glu, pw_b, dw_w, dw_b, ln_w, ln_b, bsz, t_len, tc=256, rt=32):
    m = bsz * t_len
    nt = t_len // tc
    kern = functools.partial(_cconv_kernel, tc=tc, rt=rt)
    half = lambda c: pl.BlockSpec((tc, CONV_WIDTH), lambda b, t: (b * nt + t, c))
    bias = lambda c: pl.BlockSpec((1, CONV_WIDTH), lambda b, t: (0, c))
    vec = pl.BlockSpec((1, CONV_WIDTH), lambda b, t: (0, 0))
    return pl.pallas_call(
        kern,
        grid=(bsz, nt),
        in_specs=[half(0), half(1), bias(0), bias(1),
                  pl.BlockSpec((DW_KERNEL, CONV_WIDTH), lambda b, t: (0, 0)), vec, vec, vec],
        out_specs=pl.BlockSpec((tc, CONV_WIDTH), lambda b, t: (b * nt + t, 0)),
        out_shape=jax.ShapeDtypeStruct((m, CONV_WIDTH), BF16),
        scratch_shapes=[pltpu.VMEM((tc + HIST, CONV_WIDTH), F32),
                        pltpu.VMEM((SUBLANES - 1, tc + HIST - SUBLANES, LANES), F32),
                        pltpu.VMEM((tc, CONV_WIDTH), F32)],
        compiler_params=_params(("arbitrary", "arbitrary")),
        name="conformer_conv",
    )(glu, glu, pw_b.reshape(1, -1), pw_b.reshape(1, -1), dw_w,
      dw_b.reshape(1, -1), ln_w.reshape(1, -1), ln_b.reshape(1, -1))


def _pad_lanes(v, offset):
    out = jnp.zeros((1, LANES), F32)
    return lax.dynamic_update_slice(out, v.reshape(1, -1).astype(F32), (0, offset))


def kernel(x, pre_mix_norm, w_in, gdn_conv_w, gdn_a_log, gdn_dt_bias, gdn_norm_w, cm_pw_b, cm_dw_w,
           cm_dw_b, cm_ln_w, cm_ln_b, w_out, post_mix_norm, pre_ffn_norm, w_gate, w_up, w_down,
           post_ffn_norm):
    bsz, t_len, d = x.shape
    depth = w_in.shape[0]
    m = bsz * t_len
    xf = x.reshape(m, d)
    o_z = QKV_WIDTH + GDN_WIDTH
    o_g = o_z + 2 * GDN_HEADS
    tn_in = 2 * MXU_WIDTH
    n_gdn = o_z + tn_in

    w_in_t = jnp.swapaxes(w_in, 1, 2)
    w_glu_t = w_in_t[:, o_g:, :].astype(BF16)

    h = rmsnorm_bf16(xf, pre_mix_norm[0])
    for l in range(depth):
        proj = matmul_nt(h, w_in_t, l, n_gdn, 2048, tn_in, name="in_proj_gdn", row_buffers=1)
        glu = matmul_nt(h, w_glu_t, l, 2 * CONV_WIDTH, 2048, tn_in, name="in_proj_glu")

        o_a = gdn_heads(proj, gdn_conv_w[l], _pad_lanes(gdn_a_log[l], GDN_HEADS),
                        _pad_lanes(gdn_dt_bias[l], GDN_HEADS), gdn_norm_w[l].reshape(1, HEAD_DIM),
                        bsz, t_len)
        c = conformer_conv(glu, cm_pw_b[l], cm_dw_w[l], cm_dw_b[l], cm_ln_w[l], cm_ln_b[l], bsz, t_len)

        mix = matmul_cat(o_a, c, w_out, l, 2048, tn_in, BF16, name="out_proj")
        xf, hf = residual_norm(mix, xf, post_mix_norm[l], pre_ffn_norm[l])

        act = ffn_up(hf, w_gate, w_up, l, 2048, MXU_WIDTH)
        ff = matmul(act, w_down, l, 1024, MXU_WIDTH, BF16, name="ffn_down")
        if l + 1 < depth:
            xf, h = residual_norm(ff, xf, post_ffn_norm[l], pre_mix_norm[l + 1])
        else:
            xf = residual_norm(ff, xf, post_ffn_norm[l])
    return xf.reshape(bsz, t_len, d)
```

```python
import functools

import jax
import jax.numpy as jnp
from jax import lax
from jax.experimental import pallas as pl
from jax.experimental.pallas import tpu as pltpu

D_MODEL = 4096
CHUNK = 64
HEAD_DIM = 128
GDN_WIDTH = D_MODEL // 2
GDN_HEADS = GDN_WIDTH // HEAD_DIM
QKV_WIDTH = 3 * GDN_WIDTH
SHORT_CONV = 4
CONV_WIDTH = D_MODEL - GDN_WIDTH
DW_KERNEL = 31
RMS_EPS = 1e-6
LN_EPS = 1e-5
L2_EPS = 1e-6

LANES = 128
SUBLANES = 8
MXU_WIDTH = 256
VMEM_LIMIT = 60 * 1024 * 1024

F32 = jnp.float32
BF16 = jnp.bfloat16


def _params(sem):
    return pltpu.CompilerParams(dimension_semantics=sem, vmem_limit_bytes=VMEM_LIMIT)


def _sigmoid(x):
    return 0.5 * jnp.tanh(0.5 * x) + 0.5


def _silu(x):
    hx = 0.5 * x
    return hx * jnp.tanh(hx) + hx


def _rms_kernel(x_ref, w_ref, h_ref):
    x = x_ref[...]
    r = lax.rsqrt(jnp.mean(x * x, axis=-1, keepdims=True) + RMS_EPS)
    h_ref[...] = (x * r * w_ref[...]).astype(h_ref.dtype)


def rmsnorm_bf16(x, w, tr=512):
    m, d = x.shape
    return pl.pallas_call(
        _rms_kernel,
        grid=(m // tr,),
        in_specs=[pl.BlockSpec((tr, d), lambda i: (i, 0)),
                  pl.BlockSpec((1, d), lambda i: (0, 0))],
        out_specs=pl.BlockSpec((tr, d), lambda i: (i, 0)),
        out_shape=jax.ShapeDtypeStruct((m, d), BF16),
        compiler_params=_params(("arbitrary",)),
        name="rmsnorm",
    )(x, w.reshape(1, d))


def _row_block(tm, k, buffers=2):
    return pl.BlockSpec((tm, k), lambda i, j: (i, 0), pipeline_mode=pl.Buffered(buffers))


def _dot_nt(a, b):
    return lax.dot_general(a, b, (((1,), (1,)), ((), ())), preferred_element_type=F32)


def _dot_tn(a, b):
    return lax.dot_general(a, b, (((0,), (0,)), ((), ())), preferred_element_type=F32)


def _mm_kernel(a_ref, w_ref, o_ref):
    o_ref[...] = jnp.dot(a_ref[...], w_ref[...].astype(BF16),
                         preferred_element_type=F32).astype(o_ref.dtype)


def _mm_nt_kernel(a_ref, wt_ref, o_ref):
    o_ref[...] = _dot_nt(a_ref[...], wt_ref[...].astype(BF16))


def matmul(a, w_stack, layer, tm, tn, out_dtype, name):
    m, k = a.shape
    n = w_stack.shape[-1]
    return pl.pallas_call(
        _mm_kernel,
        grid=(m // tm, n // tn),
        in_specs=[_row_block(tm, k, buffers=1),
                  pl.BlockSpec((None, k, tn), lambda i, j: (layer, 0, j))],
        out_specs=pl.BlockSpec((tm, tn), lambda i, j: (i, j)),
        out_shape=jax.ShapeDtypeStruct((m, n), out_dtype),
        compiler_params=_params(("arbitrary", "arbitrary")),
        name=name,
    )(a, w_stack)


def matmul_nt(a, wt_stack, layer, n, tm, tn, name, row_buffers=2):
    m, k = a.shape
    return pl.pallas_call(
        _mm_nt_kernel,
        grid=(m // tm, n // tn),
        in_specs=[_row_block(tm, k, row_buffers),
                  pl.BlockSpec((None, tn, k), lambda i, j: (layer, j, 0))],
        out_specs=pl.BlockSpec((tm, tn), lambda i, j: (i, j)),
        out_shape=jax.ShapeDtypeStruct((m, n), F32),
        compiler_params=_params(("arbitrary", "arbitrary")),
        name=name,
    )(a, wt_stack)


def _mm_cat_kernel(a1_ref, a2_ref, w1_ref, w2_ref, o_ref):
    acc = jnp.dot(a1_ref[...], w1_ref[...].astype(BF16), preferred_element_type=F32)
    acc += jnp.dot(a2_ref[...], w2_ref[...].astype(BF16), preferred_element_type=F32)
    o_ref[...] = acc.astype(o_ref.dtype)


def matmul_cat(a1, a2, w_stack, layer, tm, tn, out_dtype, name):
    m, k1 = a1.shape
    _, k2 = a2.shape
    n = w_stack.shape[-1]
    assert k1 == k2
    return pl.pallas_call(
        _mm_cat_kernel,
        grid=(m // tm, n // tn),
        in_specs=[_row_block(tm, k1, buffers=2), _row_block(tm, k2, buffers=2),
                  pl.BlockSpec((None, k1, tn), lambda i, j: (layer, 0, j)),
                  pl.BlockSpec((None, k2, tn), lambda i, j: (layer, 1, j))],
        out_specs=pl.BlockSpec((tm, tn), lambda i, j: (i, j)),
        out_shape=jax.ShapeDtypeStruct((m, n), out_dtype),
        compiler_params=_params(("arbitrary", "arbitrary")),
        name=name,
    )(a1, a2, w_stack, w_stack)


def _ffn_up_kernel(h_ref, wg_ref, wu_ref, o_ref):
    h = h_ref[...]
    g = jnp.dot(h, wg_ref[...].astype(BF16), preferred_element_type=F32)
    u = jnp.dot(h, wu_ref[...].astype(BF16), preferred_element_type=F32)
    o_ref[...] = (_silu(g) * u).astype(o_ref.dtype)


def ffn_up(h, wg_stack, wu_stack, layer, tm, tn):
    m, k = h.shape
    n = wg_stack.shape[-1]
    wspec = pl.BlockSpec((None, k, tn), lambda i, j: (layer, 0, j))
    return pl.pallas_call(
        _ffn_up_kernel,
        grid=(m // tm, n // tn),
        in_specs=[_row_block(tm, k, buffers=2), wspec, wspec],
        out_specs=pl.BlockSpec((tm, tn), lambda i, j: (i, j)),
        out_shape=jax.ShapeDtypeStruct((m, n), BF16),
        compiler_params=_params(("arbitrary", "arbitrary")),
        name="ffn_up",
    )(h, wg_stack, wu_stack)


def _post_kernel(y_ref, x_ref, wp_ref, wn_ref, xo_ref, h_ref):
    y = y_ref[...].astype(F32)
    r = lax.rsqrt(jnp.mean(y * y, axis=-1, keepdims=True) + RMS_EPS)
    xn = x_ref[...] + y * r * wp_ref[...]
    xo_ref[...] = xn
    r2 = lax.rsqrt(jnp.mean(xn * xn, axis=-1, keepdims=True) + RMS_EPS)
    h_ref[...] = (xn * r2 * wn_ref[...]).astype(h_ref.dtype)


def _post_last_kernel(y_ref, x_ref, wp_ref, xo_ref):
    y = y_ref[...].astype(F32)
    r = lax.rsqrt(jnp.mean(y * y, axis=-1, keepdims=True) + RMS_EPS)
    xo_ref[...] = x_ref[...] + y * r * wp_ref[...]


def residual_norm(y, x, w_post, w_next=None, tr=256):
    m, d = x.shape
    row = pl.BlockSpec((tr, d), lambda i: (i, 0))
    vec = pl.BlockSpec((1, d), lambda i: (0, 0))
    if w_next is None:
        return pl.pallas_call(
            _post_last_kernel, grid=(m // tr,),
            in_specs=[row, row, vec], out_specs=row,
            out_shape=jax.ShapeDtypeStruct((m, d), F32),
            compiler_params=_params(("arbitrary",)), name="residual_norm_last",
        )(y, x, w_post.reshape(1, d))
    return pl.pallas_call(
        _post_kernel, grid=(m // tr,),
        in_specs=[row, row, vec, vec], out_specs=[row, row],
        out_shape=[jax.ShapeDtypeStruct((m, d), F32), jax.ShapeDtypeStruct((m, d), BF16)],
        compiler_params=_params(("arbitrary",)), name="residual_norm",
    )(y, x, w_post.reshape(1, d), w_next.reshape(1, d))


STEP_CHUNKS = 4


def _dot_f32(a, b):
    return jnp.dot(a, b, precision=lax.Precision.HIGHEST, preferred_element_type=F32)


def _gdn_kernel(q_ref, k_ref, v_ref, z_ref, ba_ref, cwq_ref, cwk_ref, cwv_ref,
                alog_ref, dtb_ref, nw_ref, o_ref,
                s_ref, qpad, kpad, vpad, qa, ka, va):
    t = pl.program_id(1)
    pad0 = SUBLANES
    rows = STEP_CHUNKS * CHUNK

    @pl.when(t == 0)
    def _():
        s_ref[...] = jnp.zeros_like(s_ref)
        qpad[0:pad0, :] = jnp.zeros((pad0, GDN_WIDTH), F32)
        kpad[0:pad0, :] = jnp.zeros((pad0, GDN_WIDTH), F32)
        vpad[0:pad0, :] = jnp.zeros((pad0, GDN_WIDTH), F32)

    def conv_silu(pad, x_ref, cw_ref, dst):
        pad[pad0:pad0 + rows, :] = x_ref[...]
        acc = None
        for j in range(SHORT_CONV):
            off = pad0 - (SHORT_CONV - 1) + j
            term = pad[off:off + rows, :] * cw_ref[j:j + 1, :]
            acc = term if acc is None else acc + term
        dst[...] = _silu(acc)
        pad[0:pad0, :] = pad[rows:rows + pad0, :]

    conv_silu(qpad, q_ref, cwq_ref, qa)
    conv_silu(kpad, k_ref, cwk_ref, ka)
    conv_silu(vpad, v_ref, cwv_ref, va)

    ii = lax.broadcasted_iota(jnp.int32, (CHUNK, CHUNK), 0)
    jj = lax.broadcasted_iota(jnp.int32, (CHUNK, CHUNK), 1)
    causal = jj <= ii
    strict = jj < ii
    eye = jnp.where(ii == jj, 1.0, 0.0).astype(F32)
    level_masks = []
    s_blk = 1
    while s_blk < CHUNK:
        level_masks.append(((ii // (2 * s_blk)) == (jj // (2 * s_blk))) & ((ii // s_blk) != (jj // s_blk)))
        s_blk *= 2
    nw = nw_ref[...]

    def bdot(a, b):
        return jnp.dot(a.astype(BF16), b.astype(BF16), preferred_element_type=F32)

    tri = jnp.where(causal, 1.0, 0.0).astype(F32)

    def chunk_update(r0):
        rsl = slice(r0, r0 + CHUNK)
        ba = ba_ref[rsl, :]
        beta_all = _sigmoid(ba)
        x = ba + dtb_ref[...]
        softplus = jnp.maximum(x, 0.0) + jnp.log1p(jnp.exp(-jnp.abs(x)))
        g_all = -jnp.exp(alog_ref[...]) * softplus
        gcum = _dot_f32(tri, g_all)
        gcum_t = gcum.T

        def prep(i):
            cs = slice(i * HEAD_DIM, (i + 1) * HEAD_DIM)
            gcol = gcum[:, GDN_HEADS + i:GDN_HEADS + i + 1]
            grow = gcum_t[GDN_HEADS + i:GDN_HEADS + i + 1, :]
            bcol = beta_all[:, i:i + 1]
            glast = gcol[CHUNK - 1:CHUNK, :]
            q = qa[rsl, cs]
            k = ka[rsl, cs]
            q = q * (lax.rsqrt(jnp.sum(q * q, axis=-1, keepdims=True) + L2_EPS) * (HEAD_DIM ** -0.5))
            k = k * lax.rsqrt(jnp.sum(k * k, axis=-1, keepdims=True) + L2_EPS)
            eg = jnp.exp(gcol)
            kbeta = k * bcol
            return dict(
                decay=jnp.exp(jnp.where(causal, gcol - grow, -jnp.inf)),
                k16=k.astype(BF16), qkb=jnp.concatenate([q, kbeta], axis=0).astype(BF16),
                rhs=jnp.concatenate([va[rsl, cs] * bcol, kbeta * eg], axis=1).astype(BF16),
                qg=q * eg, kd=(k * jnp.exp(glast - gcol)).astype(BF16), dlast=jnp.exp(glast))

        pr = [prep(i) for i in range(GDN_HEADS)]
        qk_kk = [_dot_nt(p["qkb"], p["k16"]) for p in pr]
        a_intra = [(x[:CHUNK] * p["decay"]).astype(BF16) for x, p in zip(qk_kk, pr)]
        lower = [jnp.where(strict, x[CHUNK:] * p["decay"], 0.0) for x, p in zip(qk_kk, pr)]

        t_inv = [eye - jnp.where(level_masks[0], lo, 0.0) for lo in lower]
        for mask in level_masks[1:]:
            y = [bdot(jnp.where(mask, lo, 0.0), t) for lo, t in zip(lower, t_inv)]
            t_inv = [t - bdot(t, yy) for t, yy in zip(t_inv, y)]
        uw = [jnp.dot(t.astype(BF16), p["rhs"], preferred_element_type=F32) for t, p in zip(t_inv, pr)]
        wq = [jnp.concatenate([x[:, HEAD_DIM:], p["qg"]], axis=0).astype(BF16) for x, p in zip(uw, pr)]

        ws_qs = [jnp.dot(wq[i], s_ref[i].astype(BF16), preferred_element_type=F32)
                 for i in range(GDN_HEADS)]
        vn16 = [(uw[i][:, :HEAD_DIM] - ws_qs[i][:CHUNK]).astype(BF16) for i in range(GDN_HEADS)]
        av = [jnp.dot(a_intra[i], vn16[i], preferred_element_type=F32) for i in range(GDN_HEADS)]
        kv = [_dot_tn(pr[i]["kd"], vn16[i]) for i in range(GDN_HEADS)]
        for i in range(GDN_HEADS):
            cs = slice(i * HEAD_DIM, (i + 1) * HEAD_DIM)
            s_ref[i] = s_ref[i] * pr[i]["dlast"] + kv[i]
            o = ws_qs[i][CHUNK:] + av[i]
            o = o * lax.rsqrt(jnp.mean(o * o, axis=-1, keepdims=True) + RMS_EPS) * nw
            o_ref[rsl, cs] = (o * _silu(z_ref[rsl, cs])).astype(o_ref.dtype)

    for c in range(STEP_CHUNKS):
        chunk_update(c * CHUNK)


def gdn_heads(proj, conv_w, a_log_row, dt_bias_row, norm_w, bsz, t_len):
    m = bsz * t_len
    rows = STEP_CHUNKS * CHUNK
    nt = t_len // rows
    col = lambda off: pl.BlockSpec((rows, GDN_WIDTH), lambda b, t: (b * nt + t, off))
    cw = lambda off: pl.BlockSpec((SHORT_CONV, GDN_WIDTH), lambda b, t: (0, off))
    vec = pl.BlockSpec((1, LANES), lambda b, t: (0, 0))
    gate_block = (QKV_WIDTH + GDN_WIDTH) // LANES
    return pl.pallas_call(
        _gdn_kernel,
        grid=(bsz, nt),
        in_specs=[col(0), col(1), col(2), col(3),
                  pl.BlockSpec((rows, LANES), lambda b, t: (b * nt + t, gate_block)),
                  cw(0), cw(1), cw(2), vec, vec, vec],
        out_specs=pl.BlockSpec((rows, GDN_WIDTH), lambda b, t: (b * nt + t, 0)),
        out_shape=jax.ShapeDtypeStruct((m, GDN_WIDTH), BF16),
        scratch_shapes=[
            pltpu.VMEM((GDN_HEADS, HEAD_DIM, HEAD_DIM), F32),
            pltpu.VMEM((rows + SUBLANES, GDN_WIDTH), F32),
            pltpu.VMEM((rows + SUBLANES, GDN_WIDTH), F32),
            pltpu.VMEM((rows + SUBLANES, GDN_WIDTH), F32),
            pltpu.VMEM((rows, GDN_WIDTH), F32),
            pltpu.VMEM((rows, GDN_WIDTH), F32),
            pltpu.VMEM((rows, GDN_WIDTH), F32),
        ],
        compiler_params=_params(("arbitrary", "arbitrary")),
        name="gdn_heads",
    )(proj, proj, proj, proj, proj, conv_w, conv_w, conv_w, a_log_row, dt_bias_row, norm_w)


HIST = 32


def _cconv_kernel(val_ref, gate_ref, bv_ref, bg_ref, dww_ref, dwb_ref, lnw_ref, lnb_ref, o_ref,
                  pad, sh, conv, *, tc, rt):
    t = pl.program_id(1)
    lead = HIST - (DW_KERNEL - 1)
    nsh = tc + HIST - SUBLANES
    ncs = CONV_WIDTH // LANES

    @pl.when(t == 0)
    def _():
        pad[0:HIST, :] = jnp.zeros((HIST, CONV_WIDTH), F32)

    pad[HIST:HIST + tc, :] = (val_ref[...] + bv_ref[...]) * _sigmoid(gate_ref[...] + bg_ref[...])

    def stripe(cs, carry):
        c0 = pl.multiple_of(cs * LANES, LANES)
        for r in range(1, SUBLANES):
            sh[r - 1] = pad[r:r + nsh, pl.ds(c0, LANES)]
        for rr in range(tc // rt):
            acc = jnp.broadcast_to(dwb_ref[:, pl.ds(c0, LANES)], (rt, LANES))
            for j in range(DW_KERNEL):
                r = (lead + j) % SUBLANES
                row = rr * rt + lead + j - r
                if r == 0:
                    xs = pad[row:row + rt, pl.ds(c0, LANES)]
                else:
                    xs = sh[r - 1, row:row + rt, :]
                acc = acc + xs * dww_ref[j:j + 1, pl.ds(c0, LANES)]
            conv[rr * rt:(rr + 1) * rt, pl.ds(c0, LANES)] = acc
        return carry

    lax.fori_loop(0, ncs, stripe, 0)
    pad[0:HIST, :] = pad[tc:tc + HIST, :]

    ln_unroll = 4

    def rows(r, carry):
        r0s = [pl.multiple_of((r * ln_unroll + u) * rt, rt) for u in range(ln_unroll)]
        ys = [conv[pl.ds(r0, rt), :] for r0 in r0s]
        mus = [jnp.mean(y, axis=-1, keepdims=True) for y in ys]
        ycs = [y - mu for y, mu in zip(ys, mus)]
        vrs = [jnp.mean(yc * yc, axis=-1, keepdims=True) for yc in ycs]
        for r0, yc, var in zip(r0s, ycs, vrs):
            yn = yc * lax.rsqrt(var + LN_EPS) * lnw_ref[...] + lnb_ref[...]
            o_ref[pl.ds(r0, rt), :] = _silu(yn).astype(o_ref.dtype)
        return carry

    lax.fori_loop(0, tc // (rt * ln_unroll), rows, 0)


def conformer_conv(glu, pw_b, dw_w, dw_b, ln_w, ln_b, bsz, t_len, tc=256, rt=32):
    m = bsz * t_len
    nt = t_len // tc
    kern = functools.partial(_cconv_kernel, tc=tc, rt=rt)
    half = lambda c: pl.BlockSpec((tc, CONV_WIDTH), lambda b, t: (b * nt + t, c))
    bias = lambda c: pl.BlockSpec((1, CONV_WIDTH), lambda b, t: (0, c))
    vec = pl.BlockSpec((1, CONV_WIDTH), lambda b, t: (0, 0))
    return pl.pallas_call(
        kern,
        grid=(bsz, nt),
        in_specs=[half(0), half(1), bias(0), bias(1),
                  pl.BlockSpec((DW_KERNEL, CONV_WIDTH), lambda b, t: (0, 0)), vec, vec, vec],
        out_specs=pl.BlockSpec((tc, CONV_WIDTH), lambda b, t: (b * nt + t, 0)),
        out_shape=jax.ShapeDtypeStruct((m, CONV_WIDTH), BF16),
        scratch_shapes=[pltpu.VMEM((tc + HIST, CONV_WIDTH), F32),
                        pltpu.VMEM((SUBLANES - 1, tc + HIST - SUBLANES, LANES), F32),
                        pltpu.VMEM((tc, CONV_WIDTH), F32)],
        compiler_params=_params(("arbitrary", "arbitrary")),
        name="conformer_conv",
    )(glu, glu, pw_b.reshape(1, -1), pw_b.reshape(1, -1), dw_w,
      dw_b.reshape(1, -1), ln_w.reshape(1, -1), ln_b.reshape(1, -1))


def _pad_lanes(v, offset):
    out = jnp.zeros((1, LANES), F32)
    return lax.dynamic_update_slice(out, v.reshape(1, -1).astype(F32), (0, offset))


def kernel(x, pre_mix_norm, w_in, gdn_conv_w, gdn_a_log, gdn_dt_bias, gdn_norm_w, cm_pw_b, cm_dw_w,
           cm_dw_b, cm_ln_w, cm_ln_b, w_out, post_mix_norm, pre_ffn_norm, w_gate, w_up, w_down,
           post_ffn_norm):
    bsz, t_len, d = x.shape
    depth = w_in.shape[0]
    m = bsz * t_len
    xf = x.reshape(m, d)
    o_z = QKV_WIDTH + GDN_WIDTH
    o_g = o_z + 2 * GDN_HEADS
    tn_in = 2 * MXU_WIDTH
    n_gdn = o_z + tn_in

    w_in_t = jnp.swapaxes(w_in, 1, 2)
    w_glu_t = w_in_t[:, o_g:, :].astype(BF16)

    h = rmsnorm_bf16(xf, pre_mix_norm[0])
    for l in range(depth):
        proj = matmul_nt(h, w_in_t, l, n_gdn, 2048, tn_in, name="in_proj_gdn", row_buffers=1)
        glu = matmul_nt(h, w_glu_t, l, 2 * CONV_WIDTH, 2048, tn_in, name="in_proj_glu")

        o_a = gdn_heads(proj, gdn_conv_w[l], _pad_lanes(gdn_a_log[l], GDN_HEADS),
                        _pad_lanes(gdn_dt_bias[l], GDN_HEADS), gdn_norm_w[l].reshape(1, HEAD_DIM),
                        bsz, t_len)
        c = conformer_conv(glu, cm_pw_b[l], cm_dw_w[l], cm_dw_b[l], cm_ln_w[l], cm_ln_b[l], bsz, t_len)

        mix = matmul_cat(o_a, c, w_out, l, 2048, tn_in, BF16, name="out_proj")
        xf, hf = residual_norm(mix, xf, post_mix_norm[l], pre_ffn_norm[l])

        act = ffn_up(hf, w_gate, w_up, l, 2048, MXU_WIDTH)
        ff = matmul(act, w_down, l, 1024, MXU_WIDTH, BF16, name="ffn_down")
        if l + 1 < depth:
            xf, h = residual_norm(ff, xf, post_ffn_norm[l], pre_mix_norm[l + 1])
        else:
            xf = residual_norm(ff, xf, post_ffn_norm[l])
    return xf.reshape(bsz, t_len, d)
```
